```python
import jax, jax.numpy as jnp
from jax import lax
import numpy as np

D_MODEL = 1024
BATCH = 4
SEQ = 8192
DEPTH = 2
DEC_BATCH = 32
DEC_SEQ = 64
PAST_LEN = 1024

CHUNK = 64
GROUP_W = D_MODEL // 4
D_MIX = 4 * GROUP_W
N_SPLITS = 11
IN_COLS = N_SPLITS * GROUP_W
CONV_A_W = 3
CONV_B_W = 31
POOL_WINDOWS = (2, 4, 8, 16)
POOL_GROUPS = len(POOL_WINDOWS)
POOL_CH = GROUP_W // POOL_GROUPS
POOL_HIST = max(POOL_WINDOWS) - 1
N_MEM = 256
MEM_HEADS = 4
MEM_HEAD_DIM = GROUP_W // MEM_HEADS
EPS = 1e-6

kernel_name = 'hymba_conv_pool_memory_streaming_encoder'


def _rmsnorm(x, g):
    xf = x.astype(jnp.float32)
    return (xf * lax.rsqrt(jnp.mean(xf * xf, axis=-1, keepdims=True) + EPS) * g.astype(jnp.float32)).astype(x.dtype)


def _layernorm(x, g, b):
    xf = x.astype(jnp.float32)
    mu = jnp.mean(xf, axis=-1, keepdims=True)
    xc = xf - mu
    var = jnp.mean(xc * xc, axis=-1, keepdims=True)
    return (xc * lax.rsqrt(var + EPS) * g.astype(jnp.float32) + b.astype(jnp.float32)).astype(x.dtype)


def _dwconv(xpad, w):
    c = xpad.shape[-1]
    return lax.conv_general_dilated(xpad, w[:, None, :].astype(xpad.dtype), window_strides=(1,), padding='VALID',
                                    dimension_numbers=('NWC', 'WIO', 'NWC'), feature_group_count=c)


def _multiscale_trailing_mean(ppad, pos0):
    t = ppad.shape[1] - POOL_HIST
    cs = lax.cumsum(ppad.astype(jnp.float32), axis=1)
    cs = jnp.concatenate([jnp.zeros_like(cs[:, :1]), cs], axis=1)
    pos = pos0 + jnp.arange(t)
    outs = []
    for gi, w in enumerate(POOL_WINDOWS):
        sl = slice(gi * POOL_CH, (gi + 1) * POOL_CH)
        end = cs[:, POOL_HIST + 1:POOL_HIST + 1 + t, sl]
        start = cs[:, POOL_HIST + 1 - w:POOL_HIST + 1 - w + t, sl]
        cnt = jnp.minimum(w, pos + 1).astype(jnp.float32)[None, :, None]
        outs.append((end - start) / cnt)
    return jnp.concatenate(outs, axis=-1)


def _mem_kv(mem, g, w_k, w_v):
    b = mem.shape[0]
    mn = _rmsnorm(mem, g)
    k = jnp.einsum('bmd,de->bme', mn, w_k).reshape(b, N_MEM, MEM_HEADS, MEM_HEAD_DIM)
    v = jnp.einsum('bmd,de->bme', mn, w_v).reshape(b, N_MEM, MEM_HEADS, MEM_HEAD_DIM)
    return k, v


def _layer(x, hist_a, hist_b, hist_p, mem_k, mem_v, pos0, lp):
    norm_g, w_in, conv_a_w, conv_b_w, conv_b_bias, ln_b_g, ln_b_b, pool_w, pool_bias, pool_scale, w_out = lp
    b, t, _ = x.shape
    h = _rmsnorm(x, norm_g)
    u = jnp.einsum('btd,de->bte', h, w_in)
    a_b, a_c, a_x, a_g, b_v, b_a, b_g, c_u, c_g, x_q, x_g = jnp.split(u, N_SPLITS, axis=-1)

    v = a_c * a_x
    vpad = jnp.concatenate([hist_a, v], axis=1)
    y_a = a_b * _dwconv(vpad, conv_a_w) * jax.nn.silu(a_g)
    new_a = vpad[:, -(CONV_A_W - 1):]

    glu = b_v * jax.nn.sigmoid(b_a)
    gpad = jnp.concatenate([hist_b, glu], axis=1)
    z = _dwconv(gpad, conv_b_w) + conv_b_bias
    z = _layernorm(z, ln_b_g, ln_b_b)
    y_b = jax.nn.silu(z) * jax.nn.silu(b_g)
    new_b = gpad[:, -(CONV_B_W - 1):]

    ppad = jnp.concatenate([hist_p, c_u], axis=1)
    pooled = _multiscale_trailing_mean(ppad, pos0)
    d = (pooled - c_u.astype(jnp.float32)).astype(c_u.dtype).reshape(b, t, POOL_GROUPS, POOL_CH)
    mixed = jnp.einsum('btgc,gce->btge', d, pool_w).reshape(b, t, GROUP_W) + pool_bias
    y_c = mixed * pool_scale * jax.nn.silu(c_g)
    new_p = ppad[:, -POOL_HIST:]

    q = x_q.reshape(b, t, MEM_HEADS, MEM_HEAD_DIM)
    s = jnp.einsum('bthd,bmhd->bhtm', q, mem_k).astype(jnp.float32) * (MEM_HEAD_DIM ** -0.5)
    p = jax.nn.softmax(s, axis=-1).astype(mem_v.dtype)
    o = jnp.einsum('bhtm,bmhd->bthd', p, mem_v).reshape(b, t, GROUP_W)
    y_x = o * jax.nn.silu(x_g)

    y = jnp.concatenate([y_a, y_b, y_c, y_x], axis=-1)
    return x + jnp.einsum('bte,ed->btd', y, w_out), new_a, new_b, new_p


def setup_inputs(seed: int = 0) -> dict:
    key = jax.random.key(seed)
    ks = jax.random.split(key, 24)
    f = jnp.float32
    nrm = lambda k, shape, scale: jax.random.normal(k, shape, f) * scale
    return {
        'x_prompt': nrm(ks[0], (BATCH, SEQ, D_MODEL), 1.0),
        'x_sample': nrm(ks[1], (DEC_BATCH, DEC_SEQ, D_MODEL), 1.0),
        'mem_prompt': nrm(ks[2], (BATCH, N_MEM, D_MODEL), 1.0),
        'state_conv_a': nrm(ks[3], (DEPTH, DEC_BATCH, CONV_A_W - 1, GROUP_W), 1.0),
        'state_conv_b': nrm(ks[4], (DEPTH, DEC_BATCH, CONV_B_W - 1, GROUP_W), 1.0),
        'state_pool': nrm(ks[5], (DEPTH, DEC_BATCH, POOL_HIST, GROUP_W), 1.0),
        'cache_mem_k': nrm(ks[6], (DEPTH, DEC_BATCH, N_MEM, MEM_HEADS, MEM_HEAD_DIM), 1.0),
        'cache_mem_v': nrm(ks[7], (DEPTH, DEC_BATCH, N_MEM, MEM_HEADS, MEM_HEAD_DIM), 1.0),
        'norm_g': 1.0 + nrm(ks[8], (DEPTH, D_MODEL), 0.05),
        'w_in': nrm(ks[9], (DEPTH, D_MODEL, IN_COLS), D_MODEL ** -0.5),
        'conv_a_w': nrm(ks[10], (DEPTH, CONV_A_W, GROUP_W), CONV_A_W ** -0.5),
        'conv_b_w': nrm(ks[11], (DEPTH, CONV_B_W, GROUP_W), CONV_B_W ** -0.5),
        'conv_b_bias': nrm(ks[12], (DEPTH, GROUP_W), 0.02),
        'ln_b_g': 1.0 + nrm(ks[13], (DEPTH, GROUP_W), 0.05),
        'ln_b_b': nrm(ks[14], (DEPTH, GROUP_W), 0.02),
        'pool_w': nrm(ks[15], (DEPTH, POOL_GROUPS, POOL_CH, POOL_CH), POOL_CH ** -0.5),
        'pool_bias': nrm(ks[16], (DEPTH, GROUP_W), 0.02),
        'pool_scale': 1.0 + nrm(ks[17], (DEPTH, GROUP_W), 0.1),
        'mem_norm_g': 1.0 + nrm(ks[18], (DEPTH, D_MODEL), 0.05),
        'w_mem_k': nrm(ks[19], (DEPTH, D_MODEL, GROUP_W), D_MODEL ** -0.5),
        'w_mem_v': nrm(ks[20], (DEPTH, D_MODEL, GROUP_W), D_MODEL ** -0.5),
        'w_out': nrm(ks[21], (DEPTH, D_MIX, D_MODEL), D_MIX ** -0.5),
        'final_norm_g': 1.0 + nrm(ks[22], (D_MODEL,), 0.05),
    }


def reference(x_prompt, x_sample, mem_prompt, state_conv_a, state_conv_b, state_pool, cache_mem_k, cache_mem_v,
              norm_g, w_in, conv_a_w, conv_b_w, conv_b_bias, ln_b_g, ln_b_b, pool_w, pool_bias, pool_scale,
              mem_norm_g, w_mem_k, w_mem_v, w_out, final_norm_g):
    xp, xs = x_prompt, x_sample
    bp = xp.shape[0]
    na_p, nb_p, np_p, mk_p, mv_p = [], [], [], [], []
    na_s, nb_s, np_s = [], [], []
    for l in range(DEPTH):
        lp = (norm_g[l], w_in[l], conv_a_w[l], conv_b_w[l], conv_b_bias[l], ln_b_g[l], ln_b_b[l],
              pool_w[l], pool_bias[l], pool_scale[l], w_out[l])
        mk, mv = _mem_kv(mem_prompt, mem_norm_g[l], w_mem_k[l], w_mem_v[l])
        za = jnp.zeros((bp, CONV_A_W - 1, GROUP_W), xp.dtype)
        zb = jnp.zeros((bp, CONV_B_W - 1, GROUP_W), xp.dtype)
        zp = jnp.zeros((bp, POOL_HIST, GROUP_W), xp.dtype)
        xp, a1, b1, p1 = _layer(xp, za, zb, zp, mk, mv, 0, lp)
        na_p.append(a1); nb_p.append(b1); np_p.append(p1); mk_p.append(mk); mv_p.append(mv)
        xs, a2, b2, p2 = _layer(xs, state_conv_a[l], state_conv_b[l], state_pool[l],
                                cache_mem_k[l], cache_mem_v[l], PAST_LEN, lp)
        na_s.append(a2); nb_s.append(b2); np_s.append(p2)
    y_prompt = _rmsnorm(xp, final_norm_g)
    y_sample = _rmsnorm(xs, final_norm_g)
    return (y_prompt, y_sample,
            jnp.stack(na_p), jnp.stack(nb_p), jnp.stack(np_p), jnp.stack(mk_p), jnp.stack(mv_p),
            jnp.stack(na_s), jnp.stack(nb_s), jnp.stack(np_s))
```

```python
import functools

import jax
import jax.numpy as jnp
from jax import lax
from jax.experimental import pallas as pl
from jax.experimental.pallas import tpu as pltpu

D_MODEL = 1024
GROUP_W = 256
IN_COLS = 11 * GROUP_W
CONV_A_W = 3
CONV_B_W = 31
POOL_WINDOWS = (2, 4, 8, 16)
POOL_CH = 64
POOL_HIST = 15
N_MEM = 256
MEM_HEADS = 4
MEM_HEAD_DIM = 64
EPS = 1e-6
PAST_LEN = 1024

COL_A_B, COL_A_C, COL_A_X, COL_A_G, COL_B_V, COL_B_A, COL_B_G, COL_C_U, COL_C_G, COL_X_Q, COL_X_G = range(11)

PAD_A = 8
PAD_B = 32
PAD_P = 16

ROW_CHUNK = 32
VMEM_LIMIT_BYTES = 56 * 1024 * 1024

F32 = jnp.float32
BF16 = jnp.bfloat16


def _cols(g):
    return slice(g * GROUP_W, (g + 1) * GROUP_W)


def _sigmoid(x):
    return 1.0 / (1.0 + jnp.exp(-x))


def _silu(x):
    return x * _sigmoid(x)


def _layer_kernel(x_ref, sa_ref, sb_ref, sp_ref, k_ref, v_ref, ng_ref, win_ref, caw_ref, cbw_ref, cbb_ref,
                  lng_ref, lnb_ref, pw_ref, pb_ref, ps_ref, wout_ref, fg_ref,
                  y_ref, na_ref, nb_ref, np_ref,
                  ga, gb, gp, hbuf, ubuf, dbuf, sbuf, pbuf, ybuf, kbd, vbd,
                  *, nb, tt, pos0, final_norm):
    t = pl.program_id(1)
    nt = pl.num_programs(1)
    m = nb * tt

    @pl.when(t == 0)
    def _():
        ga[:, PAD_A - 2:PAD_A, :] = sa_ref[...]
        gb[:, PAD_B - 30:PAD_B, :] = sb_ref[...]
        gp[:, PAD_P - 15:PAD_P, :] = sp_ref[...]
        row_head = lax.broadcasted_iota(jnp.int32, (GROUP_W, N_MEM), 0) // MEM_HEAD_DIM
        lane_head = lax.broadcasted_iota(jnp.int32, (N_MEM, GROUP_W), 1) // MEM_HEAD_DIM
        for r in range(nb):
            k_t = k_ref[r].T
            v_r = v_ref[r]
            for h in range(MEM_HEADS):
                kbd[r, :, h * N_MEM:(h + 1) * N_MEM] = jnp.where(row_head == h, k_t, 0.0).astype(BF16)
                vbd[r, h * N_MEM:(h + 1) * N_MEM, :] = jnp.where(lane_head == h, v_r, 0.0).astype(BF16)

    for c0 in range(0, m, ROW_CHUNK):
        rows = slice(c0, c0 + ROW_CHUNK)
        r, q0 = divmod(c0, tt)
        x = x_ref[r, q0:q0 + ROW_CHUNK, :]
        ms = jnp.mean(x * x, axis=-1, keepdims=True)
        hbuf[rows, :] = (x * lax.rsqrt(ms + EPS) * ng_ref[...]).astype(BF16)

    ubuf[...] = jnp.dot(hbuf[...], win_ref[...], preferred_element_type=F32)

    lane = lax.broadcasted_iota(jnp.int32, (ROW_CHUNK, 128), 1)
    low_half = lane < POOL_CH
    for r in range(nb):
        for q0 in range(0, tt, ROW_CHUNK):
            rows = slice(r * tt + q0, r * tt + q0 + ROW_CHUNK)

            v = ubuf[rows, _cols(COL_A_C)] * ubuf[rows, _cols(COL_A_X)]
            ga[r, PAD_A + q0:PAD_A + q0 + ROW_CHUNK, :] = v
            acc = caw_ref[2:3, :] * v
            for k in range(CONV_A_W - 1):
                o = PAD_A - 2 + k + q0
                acc = acc + caw_ref[k:k + 1, :] * ga[r, o:o + ROW_CHUNK, :]
            y_a = ubuf[rows, _cols(COL_A_B)] * acc * _silu(ubuf[rows, _cols(COL_A_G)])
            ybuf[rows, _cols(0)] = y_a.astype(BF16)

            glu = ubuf[rows, _cols(COL_B_V)] * _sigmoid(ubuf[rows, _cols(COL_B_A)])
            gb[r, PAD_B + q0:PAD_B + q0 + ROW_CHUNK, :] = glu
            acc = cbw_ref[CONV_B_W - 1:CONV_B_W, :] * glu
            for k in range(CONV_B_W - 1):
                o = PAD_B - 30 + k + q0
                acc = acc + cbw_ref[k:k + 1, :] * gb[r, o:o + ROW_CHUNK, :]
            z = acc + cbb_ref[...]
            mu = jnp.mean(z, axis=-1, keepdims=True)
            zc = z - mu
            var = jnp.mean(zc * zc, axis=-1, keepdims=True)
            zn = zc * lax.rsqrt(var + EPS) * lng_ref[...] + lnb_ref[...]
            y_b = _silu(zn) * _silu(ubuf[rows, _cols(COL_B_G)])
            ybuf[rows, _cols(1)] = y_b.astype(BF16)

            cu = ubuf[rows, _cols(COL_C_U)]
            gp[r, PAD_P + q0:PAD_P + q0 + ROW_CHUNK, :] = cu
            pos = pos0 + t * tt + q0 + lax.broadcasted_iota(jnp.int32, (ROW_CHUNK, 1), 0)
            inv = [1.0 / jnp.minimum(w, pos + 1).astype(F32) for w in POOL_WINDOWS]

            def hist(j, lanes):
                o = PAD_P - j + q0
                return gp[r, o:o + ROW_CHUNK, lanes]

            lo, hi = slice(0, 128), slice(128, 256)
            s2 = cu[:, lo] + hist(1, lo)
            s4 = s2 + hist(2, lo) + hist(3, lo)
            s8 = cu[:, hi]
            for j in range(1, 8):
                s8 = s8 + hist(j, hi)
            s16 = s8
            for j in range(8, 16):
                s16 = s16 + hist(j, hi)
            pooled_lo = jnp.where(low_half, s2 * inv[0], s4 * inv[1])
            pooled_hi = jnp.where(low_half, s8 * inv[2], s16 * inv[3])
            dbuf[rows, 0:128] = (pooled_lo - cu[:, lo]).astype(BF16)
            dbuf[rows, 128:256] = (pooled_hi - cu[:, hi]).astype(BF16)

    ubuf[:, _cols(COL_C_U)] = jnp.dot(dbuf[...], pw_ref[...], preferred_element_type=F32)

    for r in range(nb):
        rows = slice(r * tt, (r + 1) * tt)
        qb = (ubuf[rows, _cols(COL_X_Q)] * (MEM_HEAD_DIM ** -0.5)).astype(BF16)
        sbuf[rows, :] = jnp.dot(qb, kbd[r], preferred_element_type=F32)

    for c0 in range(0, m, ROW_CHUNK):
        rows = slice(c0, c0 + ROW_CHUNK)
        for h in range(MEM_HEADS):
            hc = slice(h * N_MEM, (h + 1) * N_MEM)
            s = sbuf[rows, hc]
            e = jnp.exp(s - jnp.max(s, axis=-1, keepdims=True))
            pbuf[rows, hc] = (e * (1.0 / jnp.sum(e, axis=-1, keepdims=True))).astype(BF16)

    for r in range(nb):
        rows = slice(r * tt, (r + 1) * tt)
        ubuf[rows, _cols(COL_X_Q)] = jnp.dot(pbuf[rows, :], vbd[r], preferred_element_type=F32)

    for c0 in range(0, m, ROW_CHUNK):
        rows = slice(c0, c0 + ROW_CHUNK)
        y_c = (ubuf[rows, _cols(COL_C_U)] + pb_ref[...]) * ps_ref[...] * _silu(ubuf[rows, _cols(COL_C_G)])
        ybuf[rows, _cols(2)] = y_c.astype(BF16)
        y_x = ubuf[rows, _cols(COL_X_Q)] * _silu(ubuf[rows, _cols(COL_X_G)])
        ybuf[rows, _cols(3)] = y_x.astype(BF16)

    ubuf[:, 0:D_MODEL] = jnp.dot(ybuf[...], wout_ref[...], preferred_element_type=F32)
    for c0 in range(0, m, ROW_CHUNK):
        rows = slice(c0, c0 + ROW_CHUNK)
        r, q0 = divmod(c0, tt)
        out = x_ref[r, q0:q0 + ROW_CHUNK, :] + ubuf[rows, 0:D_MODEL]
        if final_norm:
            ms = jnp.mean(out * out, axis=-1, keepdims=True)
            out = out * lax.rsqrt(ms + EPS) * fg_ref[...]
        y_ref[r, q0:q0 + ROW_CHUNK, :] = out

    tail_a = ga[:, PAD_A + tt - 2:PAD_A + tt, :]
    tail_b = gb[:, PAD_B + tt - 30:PAD_B + tt, :]
    tail_p = gp[:, PAD_P + tt - 15:PAD_P + tt, :]
    ga[:, PAD_A - 2:PAD_A, :] = tail_a
    gb[:, PAD_B - 30:PAD_B, :] = tail_b
    gp[:, PAD_P - 15:PAD_P, :] = tail_p

    @pl.when(t == nt - 1)
    def _():
        na_ref[...] = tail_a
        nb_ref[...] = tail_b
        np_ref[...] = tail_p


def _layer_call(x, sa, sb, sp, mk, mv, lp, final_g, *, nb, tt, pos0, final_norm):
    b, t_len, _ = x.shape
    grid = (b // nb, t_len // tt)
    m = nb * tt
    norm_g, w_in, conv_a_w, conv_b_w, conv_b_bias, ln_g, ln_b, pool_bd, pool_bias, pool_scale, w_out = lp

    def per_batch(shape):
        return pl.BlockSpec((nb,) + shape, lambda i, j: (i,) + (0,) * len(shape))

    def whole(a):
        return pl.BlockSpec(a.shape, lambda i, j: (0,) * a.ndim)

    params = (norm_g, w_in, conv_a_w, conv_b_w, conv_b_bias, ln_g, ln_b, pool_bd, pool_bias, pool_scale, w_out, final_g)
    in_specs = [pl.BlockSpec((nb, tt, D_MODEL), lambda i, j: (i, j, 0)),
                per_batch((CONV_A_W - 1, GROUP_W)), per_batch((CONV_B_W - 1, GROUP_W)), per_batch((POOL_HIST, GROUP_W)),
                per_batch((N_MEM, GROUP_W)), per_batch((N_MEM, GROUP_W))] + [whole(p) for p in params]
    out_specs = [pl.BlockSpec((nb, tt, D_MODEL), lambda i, j: (i, j, 0)),
                 per_batch((CONV_A_W - 1, GROUP_W)), per_batch((CONV_B_W - 1, GROUP_W)), per_batch((POOL_HIST, GROUP_W))]
    out_shape = [jax.ShapeDtypeStruct(x.shape, F32),
                 jax.ShapeDtypeStruct((b, CONV_A_W - 1, GROUP_W), F32),
                 jax.ShapeDtypeStruct((b, CONV_B_W - 1, GROUP_W), F32),
                 jax.ShapeDtypeStruct((b, POOL_HIST, GROUP_W), F32)]
    scratch = [pltpu.VMEM((nb, PAD_A + tt, GROUP_W), F32),
               pltpu.VMEM((nb, PAD_B + tt, GROUP_W), F32),
               pltpu.VMEM((nb, PAD_P + tt, GROUP_W), F32),
               pltpu.VMEM((m, D_MODEL), BF16),
               pltpu.VMEM((m, IN_COLS), F32),
               pltpu.VMEM((m, GROUP_W), BF16),
               pltpu.VMEM((m, MEM_HEADS * N_MEM), F32),
               pltpu.VMEM((m, MEM_HEADS * N_MEM), BF16),
               pltpu.VMEM((m, D_MODEL), BF16),
               pltpu.VMEM((nb, GROUP_W, MEM_HEADS * N_MEM), BF16),
               pltpu.VMEM((nb, MEM_HEADS * N_MEM, GROUP_W), BF16)]
    body = functools.partial(_layer_kernel, nb=nb, tt=tt, pos0=pos0, final_norm=final_norm)
    return pl.pallas_call(
        body, grid=grid, in_specs=in_specs, out_specs=out_specs, out_shape=out_shape, scratch_shapes=scratch,
        compiler_params=pltpu.CompilerParams(dimension_semantics=("parallel", "arbitrary"),
                                             vmem_limit_bytes=VMEM_LIMIT_BYTES),
        name="encoder_layer",
    )(x, sa, sb, sp, mk, mv, *params)


def _memkv_kernel(mem_ref, g_ref, wk_ref, wv_ref, k_ref, v_ref):
    x = mem_ref[...]
    ms = jnp.mean(x * x, axis=-1, keepdims=True)
    mn = (x * lax.rsqrt(ms + EPS) * g_ref[0]).astype(BF16)
    k_ref[0] = jnp.dot(mn, wk_ref[0], preferred_element_type=F32)
    v_ref[0] = jnp.dot(mn, wv_ref[0], preferred_element_type=F32)


def _memkv_call(mem2d, g, wk, wv):
    depth = g.shape[0]
    rows = mem2d.shape[0]
    out = jax.ShapeDtypeStruct((depth, rows, GROUP_W), F32)
    return pl.pallas_call(
        _memkv_kernel, grid=(depth,),
        in_specs=[pl.BlockSpec((rows, D_MODEL), lambda l: (0, 0)),
                  pl.BlockSpec((1, 1, D_MODEL), lambda l: (l, 0, 0)),
                  pl.BlockSpec((1, D_MODEL, GROUP_W), lambda l: (l, 0, 0)),
                  pl.BlockSpec((1, D_MODEL, GROUP_W), lambda l: (l, 0, 0))],
        out_specs=[pl.BlockSpec((1, rows, GROUP_W), lambda l: (l, 0, 0)),
                   pl.BlockSpec((1, rows, GROUP_W), lambda l: (l, 0, 0))],
        out_shape=[out, out],
        compiler_params=pltpu.CompilerParams(dimension_semantics=("arbitrary",), vmem_limit_bytes=VMEM_LIMIT_BYTES),
        name="memory_kv",
    )(mem2d, g, wk, wv)


def _pool_block_diag(pool_w):
    g = pool_w.shape[0]
    eye = jnp.eye(g, dtype=pool_w.dtype)
    return jnp.einsum('gce,gh->gche', pool_w, eye).reshape(g * POOL_CH, g * POOL_CH)


def kernel(x_prompt, x_sample, mem_prompt, state_conv_a, state_conv_b, state_pool, cache_mem_k, cache_mem_v,
           norm_g, w_in, conv_a_w, conv_b_w, conv_b_bias, ln_b_g, ln_b_b, pool_w, pool_bias, pool_scale,
           mem_norm_g, w_mem_k, w_mem_v, w_out, final_norm_g):
    depth = norm_g.shape[0]
    bp, seq, _ = x_prompt.shape
    bs, dec_seq, _ = x_sample.shape

    mk_all, mv_all = _memkv_call(mem_prompt.reshape(bp * N_MEM, D_MODEL), mem_norm_g[:, None, :],
                                 w_mem_k.astype(BF16), w_mem_v.astype(BF16))
    mk_all = mk_all.reshape(depth, bp, N_MEM, GROUP_W)
    mv_all = mv_all.reshape(depth, bp, N_MEM, GROUP_W)
    cache_k = cache_mem_k.reshape(depth, bs, N_MEM, GROUP_W)
    cache_v = cache_mem_v.reshape(depth, bs, N_MEM, GROUP_W)

    w_in_b = w_in.astype(BF16)
    w_out_b = w_out.astype(BF16)
    final_g = final_norm_g[None, :]
    za = jnp.zeros((bp, CONV_A_W - 1, GROUP_W), F32)
    zb = jnp.zeros((bp, CONV_B_W - 1, GROUP_W), F32)
    zp = jnp.zeros((bp, POOL_HIST, GROUP_W), F32)

    xp, xs = x_prompt, x_sample
    na_p, nb_p, np_p, na_s, nb_s, np_s = [], [], [], [], [], []
    for l in range(depth):
        last = l == depth - 1
        lp = (norm_g[l][None, :], w_in_b[l], conv_a_w[l], conv_b_w[l], conv_b_bias[l][None, :], ln_b_g[l][None, :],
              ln_b_b[l][None, :], _pool_block_diag(pool_w[l]).astype(BF16), pool_bias[l][None, :],
              pool_scale[l][None, :], w_out_b[l])
        xp, a1, b1, p1 = _layer_call(xp, za, zb, zp, mk_all[l], mv_all[l], lp, final_g,
                                     nb=1, tt=512, pos0=0, final_norm=last)
        xs, a2, b2, p2 = _layer_call(xs, state_conv_a[l], state_conv_b[l], state_pool[l], cache_k[l], cache_v[l], lp,
                                     final_g, nb=8, tt=dec_seq, pos0=PAST_LEN, final_norm=last)
        na_p.append(a1); nb_p.append(b1); np_p.append(p1)
        na_s.append(a2); nb_s.append(b2); np_s.append(p2)

    mk_out = mk_all.reshape(depth, bp, N_MEM, MEM_HEADS, MEM_HEAD_DIM)
    mv_out = mv_all.reshape(depth, bp, N_MEM, MEM_HEADS, MEM_HEAD_DIM)
    return (xp, xs, jnp.stack(na_p), jnp.stack(nb_p), jnp.stack(np_p), mk_out, mv_out,
            jnp.stack(na_s), jnp.stack(nb_s), jnp.stack(np_s))
```

```python
import functools

import jax
import jax.numpy as jnp
from jax import lax
from jax.experimental import pallas as pl
from jax.experimental.pallas import tpu as pltpu

D_MODEL = 1024
GROUP_W = 256
IN_COLS = 11 * GROUP_W
CONV_A_W = 3
CONV_B_W = 31
POOL_WINDOWS = (2, 4, 8, 16)
POOL_CH = 64
HIST_A = CONV_A_W - 1
HIST_B = CONV_B_W - 1
HIST_P = max(POOL_WINDOWS) - 1
N_MEM = 256
MEM_HEADS = 4
MEM_HEAD_DIM = 64
EPS = 1e-6
PAST_LEN = 1024

COL_A_B, COL_A_C, COL_A_X, COL_A_G, COL_B_V, COL_B_A, COL_B_G, COL_C_U, COL_C_G, COL_X_Q, COL_X_G = range(11)

LANES = 128
SUBLANES = 8
SLABS = GROUP_W // LANES
PAD_A = 8
PAD_B = 32
PAD_P = 16
ROW_CHUNK = 32
GATHER_STRIDE = ROW_CHUNK // SUBLANES
TILE_ROWS = 512
VMEM_LIMIT_BYTES = 56 * 1024 * 1024

F32 = jnp.float32
BF16 = jnp.bfloat16


def _cols(g):
    return slice(g * GROUP_W, (g + 1) * GROUP_W)


def _sigmoid(x):
    return 1.0 / (1.0 + jnp.exp(-x))


def _silu(x):
    return x * _sigmoid(x)


def _gather(slab, base):
    return slab[pl.ds(base, SUBLANES, stride=GATHER_STRIDE), :]


def _layer_kernel(*refs, nb, tt, pos0, has_state, final_norm):
    if has_state:
        x_ref, sa_ref, sb_ref, sp_ref, k_ref, v_ref = refs[:6]
        rest = refs[6:]
    else:
        x_ref, k_ref, v_ref = refs[:3]
        sa_ref = sb_ref = sp_ref = None
        rest = refs[3:]
    (ng_ref, win_ref, caw_ref, cbw_ref, cbb_ref, lng_ref, lnb_ref, pw_ref, pb_ref, ps_ref, wout_ref, fg_ref,
     y_ref, na_ref, nb_ref, np_ref,
     aext, bext, pext, wa, wb, mixed, hbuf, ubuf, dbuf, sbuf, pbuf, ybuf, kbd, vbd) = rest
    t = pl.program_id(1)
    nt = pl.num_programs(1)
    m = nb * tt
    hist = ((aext, sa_ref, na_ref, PAD_A, HIST_A), (bext, sb_ref, nb_ref, PAD_B, HIST_B),
            (pext, sp_ref, np_ref, PAD_P, HIST_P))

    @pl.when(t == 0)
    def _():
        for ext, st_ref, _, pad, hn in hist:
            for r in range(nb):
                for c in range(SLABS):
                    lanes = slice(c * LANES, (c + 1) * LANES)
                    ext[r, c, pad - hn:pad, :] = (st_ref[r, :, lanes] if has_state else jnp.zeros((hn, LANES), F32))
        for k in range(CONV_A_W):
            wa[k] = jnp.broadcast_to(caw_ref[k:k + 1, :], (SUBLANES, GROUP_W))
        for k in range(CONV_B_W):
            wb[k] = jnp.broadcast_to(cbw_ref[k:k + 1, :], (SUBLANES, GROUP_W))
        row_head = lax.broadcasted_iota(jnp.int32, (GROUP_W, N_MEM), 0) // MEM_HEAD_DIM
        lane_head = lax.broadcasted_iota(jnp.int32, (N_MEM, GROUP_W), 1) // MEM_HEAD_DIM
        for r in range(nb):
            k_t = k_ref[r].T
            v_r = v_ref[r]
            for h in range(MEM_HEADS):
                kbd[r, :, h * N_MEM:(h + 1) * N_MEM] = jnp.where(row_head == h, k_t, 0.0).astype(BF16)
                vbd[r, h * N_MEM:(h + 1) * N_MEM, :] = jnp.where(lane_head == h, v_r, 0.0).astype(BF16)

    for c0 in range(0, m, ROW_CHUNK):
        r, q0 = divmod(c0, tt)
        x = x_ref[r, q0:q0 + ROW_CHUNK, :]
        ms = jnp.mean(x * x, axis=-1, keepdims=True)
        hbuf[c0:c0 + ROW_CHUNK, :] = (x * lax.rsqrt(ms + EPS) * ng_ref[...]).astype(BF16)

    ubuf[...] = jnp.dot(hbuf[...], win_ref[...], preferred_element_type=F32)

    for r in range(nb):
        for q0 in range(0, tt, ROW_CHUNK):
            rows = slice(r * tt + q0, r * tt + q0 + ROW_CHUNK)
            v = ubuf[rows, _cols(COL_A_C)] * ubuf[rows, _cols(COL_A_X)]
            glu = ubuf[rows, _cols(COL_B_V)] * _sigmoid(ubuf[rows, _cols(COL_B_A)])
            cu = ubuf[rows, _cols(COL_C_U)]
            for c in range(SLABS):
                lanes = slice(c * LANES, (c + 1) * LANES)
                aext[r, c, PAD_A + q0:PAD_A + q0 + ROW_CHUNK, :] = v[:, lanes]
                bext[r, c, PAD_B + q0:PAD_B + q0 + ROW_CHUNK, :] = glu[:, lanes]
                pext[r, c, PAD_P + q0:PAD_P + q0 + ROW_CHUNK, :] = cu[:, lanes]

    sub_pos = pos0 + t * tt + GATHER_STRIDE * lax.broadcasted_iota(jnp.int32, (SUBLANES, 1), 0)
    low_half = lax.broadcasted_iota(jnp.int32, (SUBLANES, LANES), 1) < POOL_CH
    steps = range(GATHER_STRIDE)
    for r in range(nb):
        for q0 in range(0, tt, ROW_CHUNK):
            out_row = r * tt + q0
            for c in range(SLABS):
                lanes = slice(c * LANES, (c + 1) * LANES)

                acc = [None] * GATHER_STRIDE
                for k in range(CONV_A_W):
                    w = wa[k, :, lanes]
                    for j in steps:
                        term = w * _gather(aext.at[r, c], PAD_A - HIST_A + k + q0 + j)
                        acc[j] = term if k == 0 else acc[j] + term
                for j in steps:
                    mixed[0, c, pl.ds(out_row + j, SUBLANES, stride=GATHER_STRIDE), :] = acc[j]

                acc = [None] * GATHER_STRIDE
                for k in range(CONV_B_W):
                    w = wb[k, :, lanes]
                    for j in steps:
                        term = w * _gather(bext.at[r, c], PAD_B - HIST_B + k + q0 + j)
                        acc[j] = term if k == 0 else acc[j] + term
                for j in steps:
                    mixed[1, c, pl.ds(out_row + j, SUBLANES, stride=GATHER_STRIDE), :] = acc[j] + cbb_ref[:, lanes]

                w_near, w_far = POOL_WINDOWS[2 * c], POOL_WINDOWS[2 * c + 1]
                for j in steps:
                    cu = _gather(pext.at[r, c], PAD_P + q0 + j)
                    near = cu
                    for d in range(1, w_near):
                        near = near + _gather(pext.at[r, c], PAD_P + q0 + j - d)
                    far = near
                    for d in range(w_near, w_far):
                        far = far + _gather(pext.at[r, c], PAD_P + q0 + j - d)
                    pos = sub_pos + (q0 + j)
                    inv_near = 1.0 / jnp.minimum(w_near, pos + 1).astype(F32)
                    inv_far = 1.0 / jnp.minimum(w_far, pos + 1).astype(F32)
                    pooled = jnp.where(low_half, near * inv_near, far * inv_far)
                    mixed[2, c, pl.ds(out_row + j, SUBLANES, stride=GATHER_STRIDE), :] = pooled - cu

    for c0 in range(0, m, ROW_CHUNK):
        rows = slice(c0, c0 + ROW_CHUNK)

        def mixed_rows(g):
            return jnp.concatenate([mixed[g, c, rows, :] for c in range(SLABS)], axis=-1)

        y_a = ubuf[rows, _cols(COL_A_B)] * mixed_rows(0) * _silu(ubuf[rows, _cols(COL_A_G)])
        ybuf[rows, _cols(0)] = y_a.astype(BF16)

        z = mixed_rows(1)
        mu = jnp.mean(z, axis=-1, keepdims=True)
        zc = z - mu
        var = jnp.mean(zc * zc, axis=-1, keepdims=True)
        zn = zc * lax.rsqrt(var + EPS) * lng_ref[...] + lnb_ref[...]
        y_b = _silu(zn) * _silu(ubuf[rows, _cols(COL_B_G)])
        ybuf[rows, _cols(1)] = y_b.astype(BF16)

        dbuf[rows, :] = mixed_rows(2).astype(BF16)

    ubuf[:, _cols(COL_C_U)] = jnp.dot(dbuf[...], pw_ref[...], preferred_element_type=F32)

    for r in range(nb):
        rows = slice(r * tt, (r + 1) * tt)
        qb = (ubuf[rows, _cols(COL_X_Q)] * (MEM_HEAD_DIM ** -0.5)).astype(BF16)
        sbuf[rows, :] = jnp.dot(qb, kbd[r], preferred_element_type=F32)

    for c0 in range(0, m, ROW_CHUNK):
        rows = slice(c0, c0 + ROW_CHUNK)
        for h in range(MEM_HEADS):
            hc = slice(h * N_MEM, (h + 1) * N_MEM)
            s = sbuf[rows, hc]
            e = jnp.exp(s - jnp.max(s, axis=-1, keepdims=True))
            pbuf[rows, hc] = (e * (1.0 / jnp.sum(e, axis=-1, keepdims=True))).astype(BF16)

    for r in range(nb):
        rows = slice(r * tt, (r + 1) * tt)
        ubuf[rows, _cols(COL_X_Q)] = jnp.dot(pbuf[rows, :], vbd[r], preferred_element_type=F32)

    for c0 in range(0, m, ROW_CHUNK):
        rows = slice(c0, c0 + ROW_CHUNK)
        y_c = (ubuf[rows, _cols(COL_C_U)] + pb_ref[...]) * ps_ref[...] * _silu(ubuf[rows, _cols(COL_C_G)])
        ybuf[rows, _cols(2)] = y_c.astype(BF16)
        y_x = ubuf[rows, _cols(COL_X_Q)] * _silu(ubuf[rows, _cols(COL_X_G)])
        ybuf[rows, _cols(3)] = y_x.astype(BF16)

    ubuf[:, 0:D_MODEL] = jnp.dot(ybuf[...], wout_ref[...], preferred_element_type=F32)
    for c0 in range(0, m, ROW_CHUNK):
        r, q0 = divmod(c0, tt)
        out = x_ref[r, q0:q0 + ROW_CHUNK, :] + ubuf[c0:c0 + ROW_CHUNK, 0:D_MODEL]
        if final_norm:
            ms = jnp.mean(out * out, axis=-1, keepdims=True)
            out = out * lax.rsqrt(ms + EPS) * fg_ref[...]
        y_ref[r, q0:q0 + ROW_CHUNK, :] = out

    for ext, _, new_ref, pad, hn in hist:
        for r in range(nb):
            for c in range(SLABS):
                tail = ext[r, c, pad + tt - hn:pad + tt, :]
                ext[r, c, pad - hn:pad, :] = tail

                @pl.when(t == nt - 1)
                def _():
                    new_ref[r, :, c * LANES:(c + 1) * LANES] = tail


def _layer_call(layer, x, states, mem_k, mem_v, params, *, nb, pos0, final_norm):
    b, t_len, _ = x.shape
    tt = TILE_ROWS // nb
    grid = (b // nb, t_len // tt)
    m = TILE_ROWS
    has_state = states is not None

    def layer_block(a):
        return pl.BlockSpec((None,) + a.shape[1:], lambda i, j: (layer,) + (0,) * (a.ndim - 1))

    def state_block(hn):
        return pl.BlockSpec((None, nb, hn, GROUP_W), lambda i, j: (layer, i, 0, 0))

    def new_state_block(hn):
        return pl.BlockSpec((nb, hn, GROUP_W), lambda i, j: (i, 0, 0))

    x_spec = pl.BlockSpec((nb, tt, D_MODEL), lambda i, j: (i, j, 0))
    kv_spec = pl.BlockSpec((None, nb, N_MEM, GROUP_W), lambda i, j: (layer, i, 0, 0))
    hists = (HIST_A, HIST_B, HIST_P)
    inputs, in_specs = [x], [x_spec]
    if has_state:
        inputs += list(states)
        in_specs += [state_block(hn) for hn in hists]
    inputs += [mem_k, mem_v]
    in_specs += [kv_spec, kv_spec]
    inputs += list(params)
    in_specs += [layer_block(p) for p in params[:-1]] + [pl.BlockSpec(params[-1].shape, lambda i, j: (0, 0))]

    out_specs = [x_spec] + [new_state_block(hn) for hn in hists]
    out_shape = [jax.ShapeDtypeStruct(x.shape, F32)] + [jax.ShapeDtypeStruct((b, hn, GROUP_W), F32) for hn in hists]
    scratch = [pltpu.VMEM((nb, SLABS, pad + tt, LANES), F32) for pad in (PAD_A, PAD_B, PAD_P)]
    scratch += [pltpu.VMEM((CONV_A_W, SUBLANES, GROUP_W), F32),
                pltpu.VMEM((CONV_B_W, SUBLANES, GROUP_W), F32),
                pltpu.VMEM((3, SLABS, m, LANES), F32),
                pltpu.VMEM((m, D_MODEL), BF16),
                pltpu.VMEM((m, IN_COLS), F32),
                pltpu.VMEM((m, GROUP_W), BF16),
                pltpu.VMEM((m, MEM_HEADS * N_MEM), F32),
                pltpu.VMEM((m, MEM_HEADS * N_MEM), BF16),
                pltpu.VMEM((m, D_MODEL), BF16),
                pltpu.VMEM((nb, GROUP_W, MEM_HEADS * N_MEM), BF16),
                pltpu.VMEM((nb, MEM_HEADS * N_MEM, GROUP_W), BF16)]
    body = functools.partial(_layer_kernel, nb=nb, tt=tt, pos0=pos0, has_state=has_state, final_norm=final_norm)
    return pl.pallas_call(
        body, grid=grid, in_specs=in_specs, out_specs=out_specs, out_shape=out_shape, scratch_shapes=scratch,
        compiler_params=pltpu.CompilerParams(dimension_semantics=("parallel", "arbitrary"),
                                             vmem_limit_bytes=VMEM_LIMIT_BYTES),
        name="encoder_layer",
    )(*inputs)


def _memkv_kernel(mem_ref, g_ref, wk_ref, wv_ref, k_ref, v_ref):
    x = mem_ref[...]
    ms = jnp.mean(x * x, axis=-1, keepdims=True)
    mn = (x * lax.rsqrt(ms + EPS) * g_ref[...]).astype(BF16)
    k_ref[...] = jnp.dot(mn, wk_ref[...], preferred_element_type=F32)
    v_ref[...] = jnp.dot(mn, wv_ref[...], preferred_element_type=F32)


def _memkv_call(mem2d, g, wk, wv):
    depth = g.shape[0]
    rows = mem2d.shape[0]
    out = jax.ShapeDtypeStruct((depth, rows, GROUP_W), F32)
    return pl.pallas_call(
        _memkv_kernel, grid=(depth,),
        in_specs=[pl.BlockSpec((rows, D_MODEL), lambda l: (0, 0)),
                  pl.BlockSpec((None, 1, D_MODEL), lambda l: (l, 0, 0)),
                  pl.BlockSpec((None, D_MODEL, GROUP_W), lambda l: (l, 0, 0)),
                  pl.BlockSpec((None, D_MODEL, GROUP_W), lambda l: (l, 0, 0))],
        out_specs=[pl.BlockSpec((None, rows, GROUP_W), lambda l: (l, 0, 0)),
                   pl.BlockSpec((None, rows, GROUP_W), lambda l: (l, 0, 0))],
        out_shape=[out, out],
        compiler_params=pltpu.CompilerParams(dimension_semantics=("arbitrary",), vmem_limit_bytes=VMEM_LIMIT_BYTES),
        name="memory_kv",
    )(mem2d, g, wk, wv)


def _pool_block_diag(pool_w):
    depth, g = pool_w.shape[:2]
    eye = jnp.eye(g, dtype=pool_w.dtype)
    return jnp.einsum('lgce,gh->lgche', pool_w, eye).reshape(depth, g * POOL_CH, g * POOL_CH)


def kernel(x_prompt, x_sample, mem_prompt, state_conv_a, state_conv_b, state_pool, cache_mem_k, cache_mem_v,
           norm_g, w_in, conv_a_w, conv_b_w, conv_b_bias, ln_b_g, ln_b_b, pool_w, pool_bias, pool_scale,
           mem_norm_g, w_mem_k, w_mem_v, w_out, final_norm_g):
    depth = norm_g.shape[0]
    bp, seq, _ = x_prompt.shape
    bs, dec_seq, _ = x_sample.shape
    sample_nb = TILE_ROWS // dec_seq
    assert seq % TILE_ROWS == 0 and TILE_ROWS % dec_seq == 0 and bs % sample_nb == 0 and dec_seq % ROW_CHUNK == 0

    mk_all, mv_all = _memkv_call(mem_prompt.reshape(bp * N_MEM, D_MODEL), mem_norm_g[:, None, :],
                                 w_mem_k.astype(BF16), w_mem_v.astype(BF16))
    mk4 = mk_all.reshape(depth, bp, N_MEM, GROUP_W)
    mv4 = mv_all.reshape(depth, bp, N_MEM, GROUP_W)
    cache_k = cache_mem_k.reshape(depth, bs, N_MEM, GROUP_W)
    cache_v = cache_mem_v.reshape(depth, bs, N_MEM, GROUP_W)

    def row(a):
        return a[:, None, :]

    params = (row(norm_g), w_in.astype(BF16), conv_a_w, conv_b_w, row(conv_b_bias), row(ln_b_g), row(ln_b_b),
              _pool_block_diag(pool_w).astype(BF16), row(pool_bias), row(pool_scale), w_out.astype(BF16),
              final_norm_g[None, :])
    states = (state_conv_a, state_conv_b, state_pool)

    xp, xs = x_prompt, x_sample
    new_p, new_s = [], []
    for l in range(depth):
        last = l == depth - 1
        xp, *st = _layer_call(l, xp, None, mk4, mv4, params, nb=1, pos0=0, final_norm=last)
        new_p.append(st)
        xs, *st = _layer_call(l, xs, states, cache_k, cache_v, params, nb=sample_nb, pos0=PAST_LEN, final_norm=last)
        new_s.append(st)

    def stack(new, i):
        return jnp.stack([st[i] for st in new])

    mk_out = mk_all.reshape(depth, bp, N_MEM, MEM_HEADS, MEM_HEAD_DIM)
    mv_out = mv_all.reshape(depth, bp, N_MEM, MEM_HEADS, MEM_HEAD_DIM)
    return (xp, xs, stack(new_p, 0), stack(new_p, 1), stack(new_p, 2), mk_out, mv_out,
            stack(new_s, 0), stack(new_s, 1), stack(new_s, 2))
```

```python
import functools

import jax
import jax.numpy as jnp
from jax import lax
from jax.experimental import pallas as pl
from jax.experimental.pallas import tpu as pltpu

D_MODEL = 1024
GROUP_W = 256
IN_COLS = 11 * GROUP_W
CONV_A_W = 3
CONV_B_W = 31
POOL_WINDOWS = (2, 4, 8, 16)
POOL_CH = 64
HIST_A = CONV_A_W - 1
HIST_B = CONV_B_W - 1
HIST_P = max(POOL_WINDOWS) - 1
N_MEM = 256
MEM_HEADS = 4
MEM_HEAD_DIM = 64
EPS = 1e-6
PAST_LEN = 1024

COL_A_B, COL_A_C, COL_A_X, COL_A_G, COL_B_V, COL_B_A, COL_B_G, COL_C_U, COL_C_G, COL_X_Q, COL_X_G = range(11)

LANES = 128
SUBLANES = 8
SLABS = GROUP_W // LANES
PAD_A = 8
PAD_B = 32
PAD_P = 16
ROW_CHUNK = 32
GATHER_STRIDE = ROW_CHUNK // SUBLANES
PROMPT_TILE_ROWS = 1024
SAMPLE_TILE_ROWS = 512
VMEM_LIMIT_BYTES = 56 * 1024 * 1024

F32 = jnp.float32
BF16 = jnp.bfloat16


def _cols(g):
    return slice(g * GROUP_W, (g + 1) * GROUP_W)


def _sigmoid(x):
    return 1.0 / (1.0 + jnp.exp(-x))


def _silu(x):
    return x * _sigmoid(x)


def _gather(slab, base):
    return slab[pl.ds(base, SUBLANES, stride=GATHER_STRIDE), :]


def _layer_kernel(*refs, nb, tt, pos0, has_state, final_norm):
    if has_state:
        x_ref, sa_ref, sb_ref, sp_ref, k_ref, v_ref = refs[:6]
        rest = refs[6:]
    else:
        x_ref, k_ref, v_ref = refs[:3]
        sa_ref = sb_ref = sp_ref = None
        rest = refs[3:]
    (ng_ref, win_ref, caw_ref, cbw_ref, cbb_ref, lng_ref, lnb_ref, pw_ref, pb_ref, ps_ref, wout_ref, fg_ref,
     y_ref, na_ref, nb_ref, np_ref,
     aext, bext, pext, wa, wb, mixed, hbuf, ubuf, dbuf, sbuf, pbuf, ybuf, kbd, vbd) = rest
    t = pl.program_id(1)
    nt = pl.num_programs(1)
    m = nb * tt
    hist = ((aext, sa_ref, na_ref, PAD_A, HIST_A), (bext, sb_ref, nb_ref, PAD_B, HIST_B),
            (pext, sp_ref, np_ref, PAD_P, HIST_P))

    @pl.when(t == 0)
    def _():
        for ext, st_ref, _, pad, hn in hist:
            for r in range(nb):
                for c in range(SLABS):
                    lanes = slice(c * LANES, (c + 1) * LANES)
                    ext[r, c, pad - hn:pad, :] = (st_ref[r, :, lanes] if has_state else jnp.zeros((hn, LANES), F32))
        for k in range(CONV_A_W):
            wa[k] = jnp.broadcast_to(caw_ref[k:k + 1, :], (SUBLANES, GROUP_W))
        for k in range(CONV_B_W):
            wb[k] = jnp.broadcast_to(cbw_ref[k:k + 1, :], (SUBLANES, GROUP_W))
        row_head = lax.broadcasted_iota(jnp.int32, (GROUP_W, N_MEM), 0) // MEM_HEAD_DIM
        lane_head = lax.broadcasted_iota(jnp.int32, (N_MEM, GROUP_W), 1) // MEM_HEAD_DIM
        for r in range(nb):
            k_t = k_ref[r].T
            v_r = v_ref[r]
            for h in range(MEM_HEADS):
                kbd[r, :, h * N_MEM:(h + 1) * N_MEM] = jnp.where(row_head == h, k_t, 0.0).astype(BF16)
                vbd[r, h * N_MEM:(h + 1) * N_MEM, :] = jnp.where(lane_head == h, v_r, 0.0).astype(BF16)

    for c0 in range(0, m, ROW_CHUNK):
        r, q0 = divmod(c0, tt)
        x = x_ref[r, q0:q0 + ROW_CHUNK, :]
        ms = jnp.mean(x * x, axis=-1, keepdims=True)
        hbuf[c0:c0 + ROW_CHUNK, :] = (x * lax.rsqrt(ms + EPS) * ng_ref[...]).astype(BF16)

    ubuf[...] = jnp.dot(hbuf[...], win_ref[...], preferred_element_type=F32)

    for r in range(nb):
        for q0 in range(0, tt, ROW_CHUNK):
            rows = slice(r * tt + q0, r * tt + q0 + ROW_CHUNK)
            v = ubuf[rows, _cols(COL_A_C)] * ubuf[rows, _cols(COL_A_X)]
            glu = ubuf[rows, _cols(COL_B_V)] * _sigmoid(ubuf[rows, _cols(COL_B_A)])
            cu = ubuf[rows, _cols(COL_C_U)]
            for c in range(SLABS):
                lanes = slice(c * LANES, (c + 1) * LANES)
                aext[r, c, PAD_A + q0:PAD_A + q0 + ROW_CHUNK, :] = v[:, lanes]
                bext[r, c, PAD_B + q0:PAD_B + q0 + ROW_CHUNK, :] = glu[:, lanes]
                pext[r, c, PAD_P + q0:PAD_P + q0 + ROW_CHUNK, :] = cu[:, lanes]

    sub_pos = pos0 + t * tt + GATHER_STRIDE * lax.broadcasted_iota(jnp.int32, (SUBLANES, 1), 0)
    low_half = lax.broadcasted_iota(jnp.int32, (SUBLANES, LANES), 1) < POOL_CH
    steps = range(GATHER_STRIDE)
    for r in range(nb):
        for q0 in range(0, tt, ROW_CHUNK):
            out_row = r * tt + q0
            for c in range(SLABS):
                lanes = slice(c * LANES, (c + 1) * LANES)

                acc = [None] * GATHER_STRIDE
                for k in range(CONV_A_W):
                    w = wa[k, :, lanes]
                    for j in steps:
                        term = w * _gather(aext.at[r, c], PAD_A - HIST_A + k + q0 + j)
                        acc[j] = term if k == 0 else acc[j] + term
                for j in steps:
                    mixed[0, c, pl.ds(out_row + j, SUBLANES, stride=GATHER_STRIDE), :] = acc[j]

                acc = [None] * GATHER_STRIDE
                for k in range(CONV_B_W):
                    w = wb[k, :, lanes]
                    for j in steps:
                        term = w * _gather(bext.at[r, c], PAD_B - HIST_B + k + q0 + j)
                        acc[j] = term if k == 0 else acc[j] + term
                for j in steps:
                    mixed[1, c, pl.ds(out_row + j, SUBLANES, stride=GATHER_STRIDE), :] = acc[j] + cbb_ref[:, lanes]

                w_near, w_far = POOL_WINDOWS[2 * c], POOL_WINDOWS[2 * c + 1]
                for j in steps:
                    cu = _gather(pext.at[r, c], PAD_P + q0 + j)
                    near = cu
                    for d in range(1, w_near):
                        near = near + _gather(pext.at[r, c], PAD_P + q0 + j - d)
                    far = near
                    for d in range(w_near, w_far):
                        far = far + _gather(pext.at[r, c], PAD_P + q0 + j - d)
                    pos = sub_pos + (q0 + j)
                    inv_near = 1.0 / jnp.minimum(w_near, pos + 1).astype(F32)
                    inv_far = 1.0 / jnp.minimum(w_far, pos + 1).astype(F32)
                    pooled = jnp.where(low_half, near * inv_near, far * inv_far)
                    mixed[2, c, pl.ds(out_row + j, SUBLANES, stride=GATHER_STRIDE), :] = pooled - cu

    for c0 in range(0, m, ROW_CHUNK):
        rows = slice(c0, c0 + ROW_CHUNK)

        def mixed_rows(g):
            return jnp.concatenate([mixed[g, c, rows, :] for c in range(SLABS)], axis=-1)

        y_a = ubuf[rows, _cols(COL_A_B)] * mixed_rows(0) * _silu(ubuf[rows, _cols(COL_A_G)])
        ybuf[rows, _cols(0)] = y_a.astype(BF16)

        z = mixed_rows(1)
        mu = jnp.mean(z, axis=-1, keepdims=True)
        zc = z - mu
        var = jnp.mean(zc * zc, axis=-1, keepdims=True)
        zn = zc * lax.rsqrt(var + EPS) * lng_ref[...] + lnb_ref[...]
        y_b = _silu(zn) * _silu(ubuf[rows, _cols(COL_B_G)])
        ybuf[rows, _cols(1)] = y_b.astype(BF16)

        dbuf[rows, :] = mixed_rows(2).astype(BF16)

    ubuf[:, _cols(COL_C_U)] = jnp.dot(dbuf[...], pw_ref[...], preferred_element_type=F32)

    for r in range(nb):
        rows = slice(r * tt, (r + 1) * tt)
        qb = (ubuf[rows, _cols(COL_X_Q)] * (MEM_HEAD_DIM ** -0.5)).astype(BF16)
        sbuf[rows, :] = jnp.dot(qb, kbd[r], preferred_element_type=F32)

    for c0 in range(0, m, ROW_CHUNK):
        rows = slice(c0, c0 + ROW_CHUNK)
        for h in range(MEM_HEADS):
            hc = slice(h * N_MEM, (h + 1) * N_MEM)
            s = sbuf[rows, hc]
            e = jnp.exp(s - jnp.max(s, axis=-1, keepdims=True))
            pbuf[rows, hc] = (e * (1.0 / jnp.sum(e, axis=-1, keepdims=True))).astype(BF16)

    for r in range(nb):
        rows = slice(r * tt, (r + 1) * tt)
        ubuf[rows, _cols(COL_X_Q)] = jnp.dot(pbuf[rows, :], vbd[r], preferred_element_type=F32)

    for c0 in range(0, m, ROW_CHUNK):
        rows = slice(c0, c0 + ROW_CHUNK)
        y_c = (ubuf[rows, _cols(COL_C_U)] + pb_ref[...]) * ps_ref[...] * _silu(ubuf[rows, _cols(COL_C_G)])
        ybuf[rows, _cols(2)] = y_c.astype(BF16)
        y_x = ubuf[rows, _cols(COL_X_Q)] * _silu(ubuf[rows, _cols(COL_X_G)])
        ybuf[rows, _cols(3)] = y_x.astype(BF16)

    ubuf[:, 0:D_MODEL] = jnp.dot(ybuf[...], wout_ref[...], preferred_element_type=F32)
    for c0 in range(0, m, ROW_CHUNK):
        r, q0 = divmod(c0, tt)
        out = x_ref[r, q0:q0 + ROW_CHUNK, :] + ubuf[c0:c0 + ROW_CHUNK, 0:D_MODEL]
        if final_norm:
            ms = jnp.mean(out * out, axis=-1, keepdims=True)
            out = out * lax.rsqrt(ms + EPS) * fg_ref[...]
        y_ref[r, q0:q0 + ROW_CHUNK, :] = out

    for ext, _, new_ref, pad, hn in hist:
        for r in range(nb):
            for c in range(SLABS):
                tail = ext[r, c, pad + tt - hn:pad + tt, :]
                ext[r, c, pad - hn:pad, :] = tail

                @pl.when(t == nt - 1)
                def _():
                    new_ref[r, :, c * LANES:(c + 1) * LANES] = tail


def _layer_call(layer, x, states, mem_k, mem_v, params, *, tile_rows, nb, pos0, final_norm):
    b, t_len, _ = x.shape
    tt = tile_rows // nb
    grid = (b // nb, t_len // tt)
    m = tile_rows
    has_state = states is not None

    def layer_block(a):
        index_map = lambda i, j: (layer,) + (0,) * (a.ndim - 1)
        if a.size // a.shape[0] >= D_MODEL * D_MODEL:
            return pl.BlockSpec((None,) + a.shape[1:], index_map, pipeline_mode=pl.Buffered(1))
        return pl.BlockSpec((None,) + a.shape[1:], index_map)

    def state_block(hn):
        return pl.BlockSpec((None, nb, hn, GROUP_W), lambda i, j: (layer, i, 0, 0))

    def new_state_block(hn):
        return pl.BlockSpec((nb, hn, GROUP_W), lambda i, j: (i, 0, 0))

    x_spec = pl.BlockSpec((nb, tt, D_MODEL), lambda i, j: (i, j, 0))
    kv_spec = pl.BlockSpec((None, nb, N_MEM, GROUP_W), lambda i, j: (layer, i, 0, 0))
    hists = (HIST_A, HIST_B, HIST_P)
    inputs, in_specs = [x], [x_spec]
    if has_state:
        inputs += list(states)
        in_specs += [state_block(hn) for hn in hists]
    inputs += [mem_k, mem_v]
    in_specs += [kv_spec, kv_spec]
    inputs += list(params)
    in_specs += [layer_block(p) for p in params[:-1]] + [pl.BlockSpec(params[-1].shape, lambda i, j: (0, 0))]

    out_specs = [x_spec] + [new_state_block(hn) for hn in hists]
    out_shape = [jax.ShapeDtypeStruct(x.shape, F32)] + [jax.ShapeDtypeStruct((b, hn, GROUP_W), F32) for hn in hists]
    scratch = [pltpu.VMEM((nb, SLABS, pad + tt, LANES), F32) for pad in (PAD_A, PAD_B, PAD_P)]
    scratch += [pltpu.VMEM((CONV_A_W, SUBLANES, GROUP_W), F32),
                pltpu.VMEM((CONV_B_W, SUBLANES, GROUP_W), F32),
                pltpu.VMEM((3, SLABS, m, LANES), F32),
                pltpu.VMEM((m, D_MODEL), BF16),
                pltpu.VMEM((m, IN_COLS), F32),
                pltpu.VMEM((m, GROUP_W), BF16),
                pltpu.VMEM((m, MEM_HEADS * N_MEM), F32),
                pltpu.VMEM((m, MEM_HEADS * N_MEM), BF16),
                pltpu.VMEM((m, D_MODEL), BF16),
                pltpu.VMEM((nb, GROUP_W, MEM_HEADS * N_MEM), BF16),
                pltpu.VMEM((nb, MEM_HEADS * N_MEM, GROUP_W), BF16)]
    body = functools.partial(_layer_kernel, nb=nb, tt=tt, pos0=pos0, has_state=has_state, final_norm=final_norm)
    return pl.pallas_call(
        body, grid=grid, in_specs=in_specs, out_specs=out_specs, out_shape=out_shape, scratch_shapes=scratch,
        compiler_params=pltpu.CompilerParams(dimension_semantics=("parallel", "arbitrary"),
                                             vmem_limit_bytes=VMEM_LIMIT_BYTES),
        name="encoder_layer",
    )(*inputs)


def _memkv_kernel(mem_ref, g_ref, wk_ref, wv_ref, k_ref, v_ref):
    x = mem_ref[...]
    ms = jnp.mean(x * x, axis=-1, keepdims=True)
    mn = (x * lax.rsqrt(ms + EPS) * g_ref[...]).astype(BF16)
    k_ref[...] = jnp.dot(mn, wk_ref[...], preferred_element_type=F32)
    v_ref[...] = jnp.dot(mn, wv_ref[...], preferred_element_type=F32)


def _memkv_call(mem2d, g, wk, wv):
    depth = g.shape[0]
    rows = mem2d.shape[0]
    out = jax.ShapeDtypeStruct((depth, rows, GROUP_W), F32)
    return pl.pallas_call(
        _memkv_kernel, grid=(depth,),
        in_specs=[pl.BlockSpec((rows, D_MODEL), lambda l: (0, 0)),
                  pl.BlockSpec((None, 1, D_MODEL), lambda l: (l, 0, 0)),
                  pl.BlockSpec((None, D_MODEL, GROUP_W), lambda l: (l, 0, 0)),
                  pl.BlockSpec((None, D_MODEL, GROUP_W), lambda l: (l, 0, 0))],
        out_specs=[pl.BlockSpec((None, rows, GROUP_W), lambda l: (l, 0, 0)),
                   pl.BlockSpec((None, rows, GROUP_W), lambda l: (l, 0, 0))],
        out_shape=[out, out],
        compiler_params=pltpu.CompilerParams(dimension_semantics=("arbitrary",), vmem_limit_bytes=VMEM_LIMIT_BYTES),
        name="memory_kv",
    )(mem2d, g, wk, wv)


def _pool_block_diag(pool_w):
    depth, g = pool_w.shape[:2]
    eye = jnp.eye(g, dtype=pool_w.dtype)
    return jnp.einsum('lgce,gh->lgche', pool_w, eye).reshape(depth, g * POOL_CH, g * POOL_CH)


def kernel(x_prompt, x_sample, mem_prompt, state_conv_a, state_conv_b, state_pool, cache_mem_k, cache_mem_v,
           norm_g, w_in, conv_a_w, conv_b_w, conv_b_bias, ln_b_g, ln_b_b, pool_w, pool_bias, pool_scale,
           mem_norm_g, w_mem_k, w_mem_v, w_out, final_norm_g):
    depth = norm_g.shape[0]
    bp, seq, _ = x_prompt.shape
    bs, dec_seq, _ = x_sample.shape
    sample_nb = SAMPLE_TILE_ROWS // dec_seq
    assert seq % PROMPT_TILE_ROWS == 0 and SAMPLE_TILE_ROWS % dec_seq == 0 and bs % sample_nb == 0
    assert dec_seq % ROW_CHUNK == 0

    mk_all, mv_all = _memkv_call(mem_prompt.reshape(bp * N_MEM, D_MODEL), mem_norm_g[:, None, :],
                                 w_mem_k.astype(BF16), w_mem_v.astype(BF16))
    mk4 = mk_all.reshape(depth, bp, N_MEM, GROUP_W)
    mv4 = mv_all.reshape(depth, bp, N_MEM, GROUP_W)
    cache_k = cache_mem_k.reshape(depth, bs, N_MEM, GROUP_W)
    cache_v = cache_mem_v.reshape(depth, bs, N_MEM, GROUP_W)

    def row(a):
        return a[:, None, :]

    params = (row(norm_g), w_in.astype(BF16), conv_a_w, conv_b_w, row(conv_b_bias), row(ln_b_g), row(ln_b_b),
              _pool_block_diag(pool_w).astype(BF16), row(pool_bias), row(pool_scale), w_out.astype(BF16),
              final_norm_g[None, :])
    states = (state_conv_a, state_conv_b, state_pool)

    xp, xs = x_prompt, x_sample
    new_p, new_s = [], []
    for l in range(depth):
        last = l == depth - 1
        xp, *st = _layer_call(l, xp, None, mk4, mv4, params, tile_rows=PROMPT_TILE_ROWS, nb=1, pos0=0,
                              final_norm=last)
        new_p.append(st)
        xs, *st = _layer_call(l, xs, states, cache_k, cache_v, params, tile_rows=SAMPLE_TILE_ROWS, nb=sample_nb,
                              pos0=PAST_LEN, final_norm=last)
        new_s.append(st)

    def stack(new, i):
        return jnp.stack([st[i] for st in new])

    mk_out = mk_all.reshape(depth, bp, N_MEM, MEM_HEADS, MEM_HEAD_DIM)
    mv_out = mv_all.reshape(depth, bp, N_MEM, MEM_HEADS, MEM_HEAD_DIM)
    return (xp, xs, stack(new_p, 0), stack(new_p, 1), stack(new_p, 2), mk_out, mv_out,
            stack(new_s, 0), stack(new_s, 1), stack(new_s, 2))
```

```python
import functools

import jax
import jax.numpy as jnp
from jax import lax
from jax.experimental import pallas as pl
from jax.experimental.pallas import tpu as pltpu

D_MODEL = 1024
GROUP_W = 256
IN_COLS = 11 * GROUP_W
CONV_A_W = 3
CONV_B_W = 31
POOL_WINDOWS = (2, 4, 8, 16)
POOL_CH = 64
HIST_A = CONV_A_W - 1
HIST_B = CONV_B_W - 1
HIST_P = max(POOL_WINDOWS) - 1
N_MEM = 256
MEM_HEADS = 4
MEM_HEAD_DIM = 64
EPS = 1e-6
PAST_LEN = 1024

PROJ_PERM = (4, 5, 1, 2, 7, 9, 0, 3, 6, 8, 10)
COL_B_V, COL_B_A, COL_A_C, COL_A_X, COL_C_U, COL_X_Q, COL_A_B, COL_A_G, COL_B_G, COL_C_G, COL_X_G = range(11)

LANES = 128
SUBLANES = 8
SLABS = GROUP_W // LANES
PAD_A = 8
PAD_B = 32
PAD_P = 16
ROW_CHUNK = 32
GATHER_STRIDE = ROW_CHUNK // SUBLANES
PROMPT_TILE_ROWS = 1024
SAMPLE_TILE_ROWS = 512
BLOCK_ROWS = 256
ATT_ROWS = 512
VMEM_LIMIT_BYTES = 56 * 1024 * 1024

F32 = jnp.float32
BF16 = jnp.bfloat16


def _cols(g):
    return slice(g * GROUP_W, (g + 1) * GROUP_W)


def _sigmoid(x):
    return 1.0 / (1.0 + jnp.exp(-x))


def _silu(x):
    return x * _sigmoid(x)


def _gather(slab, base):
    return slab[pl.ds(base, SUBLANES, stride=GATHER_STRIDE), :]


def _layer_kernel(*refs, nb, tt, pos0, has_state, final_norm):
    if has_state:
        x_ref, sa_ref, sb_ref, sp_ref, k_ref, v_ref = refs[:6]
        rest = refs[6:]
    else:
        x_ref, k_ref, v_ref = refs[:3]
        sa_ref = sb_ref = sp_ref = None
        rest = refs[3:]
    (ng_ref, win_ref, caw_ref, cbw_ref, cbb_ref, lng_ref, lnb_ref, pw_ref, pb_ref, ps_ref, wout_ref, fg_ref,
     y_ref, na_ref, nb_ref, np_ref,
     aext, bext, pext, wa, wb, mixed, hbuf, ubuf, dbuf, sbuf, pbuf, ybuf, kbd, vbd) = rest
    t = pl.program_id(1)
    nt = pl.num_programs(1)
    m = nb * tt
    hist = ((aext, sa_ref, na_ref, PAD_A, HIST_A), (bext, sb_ref, nb_ref, PAD_B, HIST_B),
            (pext, sp_ref, np_ref, PAD_P, HIST_P))

    @pl.when(t == 0)
    def _():
        for ext, st_ref, _, pad, hn in hist:
            for r in range(nb):
                for c in range(SLABS):
                    lanes = slice(c * LANES, (c + 1) * LANES)
                    ext[r, c, pad - hn:pad, :] = (st_ref[r, :, lanes] if has_state else jnp.zeros((hn, LANES), F32))
        for k in range(CONV_A_W):
            wa[k] = jnp.broadcast_to(caw_ref[k:k + 1, :], (SUBLANES, GROUP_W))
        for k in range(CONV_B_W):
            wb[k] = jnp.broadcast_to(cbw_ref[k:k + 1, :], (SUBLANES, GROUP_W))
        row_head = lax.broadcasted_iota(jnp.int32, (GROUP_W, N_MEM), 0) // MEM_HEAD_DIM
        lane_head = lax.broadcasted_iota(jnp.int32, (N_MEM, GROUP_W), 1) // MEM_HEAD_DIM
        for r in range(nb):
            k_t = k_ref[r].T
            v_r = v_ref[r]
            for h in range(MEM_HEADS):
                kbd[r, :, h * N_MEM:(h + 1) * N_MEM] = jnp.where(row_head == h, k_t, 0.0).astype(BF16)
                vbd[r, h * N_MEM:(h + 1) * N_MEM, :] = jnp.where(lane_head == h, v_r, 0.0).astype(BF16)

    for c0 in range(0, m, ROW_CHUNK):
        r, q0 = divmod(c0, tt)
        x = x_ref[r, q0:q0 + ROW_CHUNK, :]
        ms = jnp.mean(x * x, axis=-1, keepdims=True)
        hbuf[c0:c0 + ROW_CHUNK, :] = (x * lax.rsqrt(ms + EPS) * ng_ref[...]).astype(BF16)

    ubuf[...] = jnp.dot(hbuf[...], win_ref[...], preferred_element_type=F32)

    sub_pos = pos0 + t * tt + GATHER_STRIDE * lax.broadcasted_iota(jnp.int32, (SUBLANES, 1), 0)
    low_half = lax.broadcasted_iota(jnp.int32, (SUBLANES, LANES), 1) < POOL_CH
    steps = range(GATHER_STRIDE)

    att_rows = min(tt, ATT_ROWS)
    for a0 in range(0, m, att_rows):
        rows = slice(a0, a0 + att_rows)
        qb = (ubuf[rows, _cols(COL_X_Q)] * (MEM_HEAD_DIM ** -0.5)).astype(BF16)
        sbuf[rows, :] = jnp.dot(qb, kbd[a0 // tt], preferred_element_type=F32)
    for c0 in range(0, m, ROW_CHUNK):
        rows = slice(c0, c0 + ROW_CHUNK)
        for h in range(MEM_HEADS):
            hc = slice(h * N_MEM, (h + 1) * N_MEM)
            s = sbuf[rows, hc]
            e = jnp.exp(s - jnp.max(s, axis=-1, keepdims=True))
            pbuf[rows, hc] = (e * (1.0 / jnp.sum(e, axis=-1, keepdims=True))).astype(BF16)
    for a0 in range(0, m, att_rows):
        rows = slice(a0, a0 + att_rows)
        ubuf[rows, _cols(COL_X_Q)] = jnp.dot(pbuf[rows, :], vbd[a0 // tt], preferred_element_type=F32)

    for b0 in range(0, m, BLOCK_ROWS):
        blk = slice(b0, b0 + BLOCK_ROWS)
        chunks = [(c0,) + divmod(c0, tt) for c0 in range(b0, b0 + BLOCK_ROWS, ROW_CHUNK)]

        for c0, r, q0 in chunks:
            rows = slice(c0, c0 + ROW_CHUNK)
            v = ubuf[rows, _cols(COL_A_C)] * ubuf[rows, _cols(COL_A_X)]
            glu = ubuf[rows, _cols(COL_B_V)] * _sigmoid(ubuf[rows, _cols(COL_B_A)])
            cu = ubuf[rows, _cols(COL_C_U)]
            for c in range(SLABS):
                lanes = slice(c * LANES, (c + 1) * LANES)
                aext[r, c, PAD_A + q0:PAD_A + q0 + ROW_CHUNK, :] = v[:, lanes]
                bext[r, c, PAD_B + q0:PAD_B + q0 + ROW_CHUNK, :] = glu[:, lanes]
                pext[r, c, PAD_P + q0:PAD_P + q0 + ROW_CHUNK, :] = cu[:, lanes]

        for out_row, r, q0 in chunks:
            for c in range(SLABS):
                lanes = slice(c * LANES, (c + 1) * LANES)

                acc = [None] * GATHER_STRIDE
                for k in range(CONV_A_W):
                    w = wa[k, :, lanes]
                    for j in steps:
                        term = w * _gather(aext.at[r, c], PAD_A - HIST_A + k + q0 + j)
                        acc[j] = term if k == 0 else acc[j] + term
                for j in steps:
                    mixed[0, c, pl.ds(out_row + j, SUBLANES, stride=GATHER_STRIDE), :] = acc[j]

                acc = [None] * GATHER_STRIDE
                for k in range(CONV_B_W):
                    w = wb[k, :, lanes]
                    for j in steps:
                        term = w * _gather(bext.at[r, c], PAD_B - HIST_B + k + q0 + j)
                        acc[j] = term if k == 0 else acc[j] + term
                for j in steps:
                    mixed[1, c, pl.ds(out_row + j, SUBLANES, stride=GATHER_STRIDE), :] = acc[j] + cbb_ref[:, lanes]

                w_near, w_far = POOL_WINDOWS[2 * c], POOL_WINDOWS[2 * c + 1]
                for j in steps:
                    cu = _gather(pext.at[r, c], PAD_P + q0 + j)
                    near = cu
                    for d in range(1, w_near):
                        near = near + _gather(pext.at[r, c], PAD_P + q0 + j - d)
                    far = near
                    for d in range(w_near, w_far):
                        far = far + _gather(pext.at[r, c], PAD_P + q0 + j - d)
                    pos = sub_pos + (q0 + j)
                    inv_near = 1.0 / jnp.minimum(w_near, pos + 1).astype(F32)
                    inv_far = 1.0 / jnp.minimum(w_far, pos + 1).astype(F32)
                    pooled = jnp.where(low_half, near * inv_near, far * inv_far)
                    mixed[2, c, pl.ds(out_row + j, SUBLANES, stride=GATHER_STRIDE), :] = pooled - cu

        for c0, _, _ in chunks:
            rows = slice(c0, c0 + ROW_CHUNK)

            def mixed_rows(g):
                return jnp.concatenate([mixed[g, c, rows, :] for c in range(SLABS)], axis=-1)

            y_a = ubuf[rows, _cols(COL_A_B)] * mixed_rows(0) * _silu(ubuf[rows, _cols(COL_A_G)])
            ybuf[rows, _cols(0)] = y_a.astype(BF16)

            z = mixed_rows(1)
            mu = jnp.mean(z, axis=-1, keepdims=True)
            zc = z - mu
            var = jnp.mean(zc * zc, axis=-1, keepdims=True)
            zn = zc * lax.rsqrt(var + EPS) * lng_ref[...] + lnb_ref[...]
            y_b = _silu(zn) * _silu(ubuf[rows, _cols(COL_B_G)])
            ybuf[rows, _cols(1)] = y_b.astype(BF16)

            dbuf[rows, :] = mixed_rows(2).astype(BF16)

        ubuf[blk, _cols(COL_C_U)] = jnp.dot(dbuf[blk, :], pw_ref[...], preferred_element_type=F32)

        for c0, _, _ in chunks:
            rows = slice(c0, c0 + ROW_CHUNK)
            y_c = (ubuf[rows, _cols(COL_C_U)] + pb_ref[...]) * ps_ref[...] * _silu(ubuf[rows, _cols(COL_C_G)])
            ybuf[rows, _cols(2)] = y_c.astype(BF16)
            y_x = ubuf[rows, _cols(COL_X_Q)] * _silu(ubuf[rows, _cols(COL_X_G)])
            ybuf[rows, _cols(3)] = y_x.astype(BF16)

        ubuf[blk, 0:D_MODEL] = jnp.dot(ybuf[blk, :], wout_ref[...], preferred_element_type=F32)
        for c0, r, q0 in chunks:
            out = x_ref[r, q0:q0 + ROW_CHUNK, :] + ubuf[c0:c0 + ROW_CHUNK, 0:D_MODEL]
            if final_norm:
                ms = jnp.mean(out * out, axis=-1, keepdims=True)
                out = out * lax.rsqrt(ms + EPS) * fg_ref[...]
            y_ref[r, q0:q0 + ROW_CHUNK, :] = out

    for ext, _, new_ref, pad, hn in hist:
        for r in range(nb):
            for c in range(SLABS):
                tail = ext[r, c, pad + tt - hn:pad + tt, :]
                ext[r, c, pad - hn:pad, :] = tail

                @pl.when(t == nt - 1)
                def _():
                    new_ref[r, :, c * LANES:(c + 1) * LANES] = tail


def _layer_call(layer, x, states, mem_k, mem_v, params, *, tile_rows, nb, pos0, final_norm):
    b, t_len, _ = x.shape
    tt = tile_rows // nb
    grid = (b // nb, t_len // tt)
    m = tile_rows
    has_state = states is not None

    def layer_block(a):
        index_map = lambda i, j: (layer,) + (0,) * (a.ndim - 1)
        if a.size // a.shape[0] >= D_MODEL * D_MODEL:
            return pl.BlockSpec((None,) + a.shape[1:], index_map, pipeline_mode=pl.Buffered(1))
        return pl.BlockSpec((None,) + a.shape[1:], index_map)

    def state_block(hn):
        return pl.BlockSpec((None, nb, hn, GROUP_W), lambda i, j: (layer, i, 0, 0))

    def new_state_block(hn):
        return pl.BlockSpec((nb, hn, GROUP_W), lambda i, j: (i, 0, 0))

    x_spec = pl.BlockSpec((nb, tt, D_MODEL), lambda i, j: (i, j, 0))
    kv_spec = pl.BlockSpec((None, nb, N_MEM, GROUP_W), lambda i, j: (layer, i, 0, 0))
    hists = (HIST_A, HIST_B, HIST_P)
    inputs, in_specs = [x], [x_spec]
    if has_state:
        inputs += list(states)
        in_specs += [state_block(hn) for hn in hists]
    inputs += [mem_k, mem_v]
    in_specs += [kv_spec, kv_spec]
    inputs += list(params)
    in_specs += [layer_block(p) for p in params[:-1]] + [pl.BlockSpec(params[-1].shape, lambda i, j: (0, 0))]

    out_specs = [x_spec] + [new_state_block(hn) for hn in hists]
    out_shape = [jax.ShapeDtypeStruct(x.shape, F32)] + [jax.ShapeDtypeStruct((b, hn, GROUP_W), F32) for hn in hists]
    scratch = [pltpu.VMEM((nb, SLABS, pad + tt, LANES), F32) for pad in (PAD_A, PAD_B, PAD_P)]
    scratch += [pltpu.VMEM((CONV_A_W, SUBLANES, GROUP_W), F32),
                pltpu.VMEM((CONV_B_W, SUBLANES, GROUP_W), F32),
                pltpu.VMEM((3, SLABS, m, LANES), F32),
                pltpu.VMEM((m, D_MODEL), BF16),
                pltpu.VMEM((m, IN_COLS), F32),
                pltpu.VMEM((m, GROUP_W), BF16),
                pltpu.VMEM((m, MEM_HEADS * N_MEM), F32),
                pltpu.VMEM((m, MEM_HEADS * N_MEM), BF16),
                pltpu.VMEM((m, D_MODEL), BF16),
                pltpu.VMEM((nb, GROUP_W, MEM_HEADS * N_MEM), BF16),
                pltpu.VMEM((nb, MEM_HEADS * N_MEM, GROUP_W), BF16)]
    body = functools.partial(_layer_kernel, nb=nb, tt=tt, pos0=pos0, has_state=has_state, final_norm=final_norm)
    return pl.pallas_call(
        body, grid=grid, in_specs=in_specs, out_specs=out_specs, out_shape=out_shape, scratch_shapes=scratch,
        compiler_params=pltpu.CompilerParams(dimension_semantics=("parallel", "arbitrary"),
                                             vmem_limit_bytes=VMEM_LIMIT_BYTES),
        name="encoder_layer",
    )(*inputs)


def _memkv_kernel(mem_ref, g_ref, wk_ref, wv_ref, k_ref, v_ref):
    x = mem_ref[...]
    ms = jnp.mean(x * x, axis=-1, keepdims=True)
    mn = (x * lax.rsqrt(ms + EPS) * g_ref[...]).astype(BF16)
    k_ref[...] = jnp.dot(mn, wk_ref[...], preferred_element_type=F32)
    v_ref[...] = jnp.dot(mn, wv_ref[...], preferred_element_type=F32)


def _memkv_call(mem2d, g, wk, wv):
    depth = g.shape[0]
    rows = mem2d.shape[0]
    out = jax.ShapeDtypeStruct((depth, rows, GROUP_W), F32)
    return pl.pallas_call(
        _memkv_kernel, grid=(depth,),
        in_specs=[pl.BlockSpec((rows, D_MODEL), lambda l: (0, 0)),
                  pl.BlockSpec((None, 1, D_MODEL), lambda l: (l, 0, 0)),
                  pl.BlockSpec((None, D_MODEL, GROUP_W), lambda l: (l, 0, 0)),
                  pl.BlockSpec((None, D_MODEL, GROUP_W), lambda l: (l, 0, 0))],
        out_specs=[pl.BlockSpec((None, rows, GROUP_W), lambda l: (l, 0, 0)),
                   pl.BlockSpec((None, rows, GROUP_W), lambda l: (l, 0, 0))],
        out_shape=[out, out],
        compiler_params=pltpu.CompilerParams(dimension_semantics=("arbitrary",), vmem_limit_bytes=VMEM_LIMIT_BYTES),
        name="memory_kv",
    )(mem2d, g, wk, wv)


def _pool_block_diag(pool_w):
    depth, g = pool_w.shape[:2]
    eye = jnp.eye(g, dtype=pool_w.dtype)
    return jnp.einsum('lgce,gh->lgche', pool_w, eye).reshape(depth, g * POOL_CH, g * POOL_CH)


def kernel(x_prompt, x_sample, mem_prompt, state_conv_a, state_conv_b, state_pool, cache_mem_k, cache_mem_v,
           norm_g, w_in, conv_a_w, conv_b_w, conv_b_bias, ln_b_g, ln_b_b, pool_w, pool_bias, pool_scale,
           mem_norm_g, w_mem_k, w_mem_v, w_out, final_norm_g):
    depth = norm_g.shape[0]
    bp, seq, _ = x_prompt.shape
    bs, dec_seq, _ = x_sample.shape
    sample_nb = SAMPLE_TILE_ROWS // dec_seq
    assert seq % PROMPT_TILE_ROWS == 0 and SAMPLE_TILE_ROWS % dec_seq == 0 and bs % sample_nb == 0
    assert dec_seq % ROW_CHUNK == 0 and BLOCK_ROWS % dec_seq == 0
    assert PROMPT_TILE_ROWS % max(BLOCK_ROWS, ATT_ROWS) == 0 and SAMPLE_TILE_ROWS % BLOCK_ROWS == 0

    mk_all, mv_all = _memkv_call(mem_prompt.reshape(bp * N_MEM, D_MODEL), mem_norm_g[:, None, :],
                                 w_mem_k.astype(BF16), w_mem_v.astype(BF16))
    mk4 = mk_all.reshape(depth, bp, N_MEM, GROUP_W)
    mv4 = mv_all.reshape(depth, bp, N_MEM, GROUP_W)
    cache_k = cache_mem_k.reshape(depth, bs, N_MEM, GROUP_W)
    cache_v = cache_mem_v.reshape(depth, bs, N_MEM, GROUP_W)

    def row(a):
        return a[:, None, :]

    w_in_perm = w_in.reshape(depth, D_MODEL, len(PROJ_PERM), GROUP_W)[:, :, PROJ_PERM, :].reshape(w_in.shape)
    params = (row(norm_g), w_in_perm.astype(BF16), conv_a_w, conv_b_w, row(conv_b_bias), row(ln_b_g), row(ln_b_b),
              _pool_block_diag(pool_w).astype(BF16), row(pool_bias), row(pool_scale), w_out.astype(BF16),
              final_norm_g[None, :])
    states = (state_conv_a, state_conv_b, state_pool)

    xp, xs = x_prompt, x_sample
    new_p, new_s = [], []
    for l in range(depth):
        last = l == depth - 1
        xp, *st = _layer_call(l, xp, None, mk4, mv4, params, tile_rows=PROMPT_TILE_ROWS, nb=1, pos0=0,
                              final_norm=last)
        new_p.append(st)
        xs, *st = _layer_call(l, xs, states, cache_k, cache_v, params, tile_rows=SAMPLE_TILE_ROWS, nb=sample_nb,
                              pos0=PAST_LEN, final_norm=last)
        new_s.append(st)

    def stack(new, i):
        return jnp.stack([st[i] for st in new])

    mk_out = mk_all.reshape(depth, bp, N_MEM, MEM_HEADS, MEM_HEAD_DIM)
    mv_out = mv_all.reshape(depth, bp, N_MEM, MEM_HEADS, MEM_HEAD_DIM)
    return (xp, xs, stack(new_p, 0), stack(new_p, 1), stack(new_p, 2), mk_out, mv_out,
            stack(new_s, 0), stack(new_s, 1), stack(new_s, 2))
```

```python
import functools

import jax
import jax.numpy as jnp
from jax import lax
from jax.experimental import pallas as pl
from jax.experimental.pallas import tpu as pltpu

D_MODEL = 1024
GROUP_W = 256
IN_COLS = 11 * GROUP_W
CONV_A_W = 3
CONV_B_W = 31
POOL_WINDOWS = (2, 4, 8, 16)
POOL_CH = 64
HIST_A = CONV_A_W - 1
HIST_B = CONV_B_W - 1
HIST_P = max(POOL_WINDOWS) - 1
N_MEM = 256
MEM_HEADS = 4
MEM_HEAD_DIM = 64
EPS = 1e-6
PAST_LEN = 1024

PROJ_PERM = (4, 5, 1, 2, 7, 9, 0, 3, 6, 8, 10)
COL_B_V, COL_B_A, COL_A_C, COL_A_X, COL_C_U, COL_X_Q, COL_A_B, COL_A_G, COL_B_G, COL_C_G, COL_X_G = range(11)

LANES = 128
SUBLANES = 8
SLABS = GROUP_W // LANES
PAD_A = 8
PAD_B = 32
PAD_P = 16
ROW_CHUNK = 32
GATHER_STRIDE = ROW_CHUNK // SUBLANES
PROMPT_TILE_ROWS = 1024
SAMPLE_TILE_ROWS = 512
BLOCK_ROWS = 256
ATT_ROWS = 512
VMEM_LIMIT_BYTES = 56 * 1024 * 1024

F32 = jnp.float32
BF16 = jnp.bfloat16


def _cols(g):
    return slice(g * GROUP_W, (g + 1) * GROUP_W)


def _sigmoid(x):
    return 1.0 / (1.0 + jnp.exp(-x))


def _silu(x):
    return x * _sigmoid(x)


def _gather(slab, base):
    return slab[pl.ds(base, SUBLANES, stride=GATHER_STRIDE), :]


def _layer_kernel(*refs, nb, tt, pos0, has_state, final_norm):
    if has_state:
        x_ref, sa_ref, sb_ref, sp_ref, k_ref, v_ref = refs[:6]
        rest = refs[6:]
    else:
        x_ref, k_ref, v_ref = refs[:3]
        sa_ref = sb_ref = sp_ref = None
        rest = refs[3:]
    (ng_ref, win_ref, caw_ref, cbw_ref, cbb_ref, lng_ref, lnb_ref, pw_ref, pb_ref, ps_ref, wout_ref, fg_ref,
     y_ref, na_ref, nb_ref, np_ref,
     aext, bext, pext, wa, wb, wperm, mixed, hbuf, ubuf, dbuf, sbuf, pbuf, ybuf, kbd, vbd) = rest
    t = pl.program_id(1)
    nt = pl.num_programs(1)
    m = nb * tt
    hist = ((aext, sa_ref, na_ref, PAD_A, HIST_A), (bext, sb_ref, nb_ref, PAD_B, HIST_B),
            (pext, sp_ref, np_ref, PAD_P, HIST_P))

    @pl.when((pl.program_id(0) == 0) & (t == 0))
    def _():
        for j, g in enumerate(PROJ_PERM):
            wperm[:, _cols(j)] = win_ref[:, _cols(g)]

    @pl.when(t == 0)
    def _():
        for ext, st_ref, _, pad, hn in hist:
            for r in range(nb):
                for c in range(SLABS):
                    lanes = slice(c * LANES, (c + 1) * LANES)
                    ext[r, c, pad - hn:pad, :] = (st_ref[r, :, lanes] if has_state else jnp.zeros((hn, LANES), F32))
        for k in range(CONV_A_W):
            wa[k] = jnp.broadcast_to(caw_ref[k:k + 1, :], (SUBLANES, GROUP_W))
        for k in range(CONV_B_W):
            wb[k] = jnp.broadcast_to(cbw_ref[k:k + 1, :], (SUBLANES, GROUP_W))
        row_head = lax.broadcasted_iota(jnp.int32, (GROUP_W, N_MEM), 0) // MEM_HEAD_DIM
        lane_head = lax.broadcasted_iota(jnp.int32, (N_MEM, GROUP_W), 1) // MEM_HEAD_DIM
        for r in range(nb):
            k_t = k_ref[r].T
            v_r = v_ref[r]
            for h in range(MEM_HEADS):
                kbd[r, :, h * N_MEM:(h + 1) * N_MEM] = jnp.where(row_head == h, k_t, 0.0).astype(BF16)
                vbd[r, h * N_MEM:(h + 1) * N_MEM, :] = jnp.where(lane_head == h, v_r, 0.0).astype(BF16)

    for c0 in range(0, m, ROW_CHUNK):
        r, q0 = divmod(c0, tt)
        x = x_ref[r, q0:q0 + ROW_CHUNK, :]
        ms = jnp.mean(x * x, axis=-1, keepdims=True)
        hbuf[c0:c0 + ROW_CHUNK, :] = (x * lax.rsqrt(ms + EPS) * ng_ref[...]).astype(BF16)

    ubuf[...] = jnp.dot(hbuf[...], wperm[...], preferred_element_type=F32)

    sub_pos = pos0 + t * tt + GATHER_STRIDE * lax.broadcasted_iota(jnp.int32, (SUBLANES, 1), 0)
    low_half = lax.broadcasted_iota(jnp.int32, (SUBLANES, LANES), 1) < POOL_CH
    steps = range(GATHER_STRIDE)

    att_rows = min(tt, ATT_ROWS)
    for a0 in range(0, m, att_rows):
        rows = slice(a0, a0 + att_rows)
        qb = (ubuf[rows, _cols(COL_X_Q)] * (MEM_HEAD_DIM ** -0.5)).astype(BF16)
        sbuf[rows, :] = jnp.dot(qb, kbd[a0 // tt], preferred_element_type=F32)
    for c0 in range(0, m, ROW_CHUNK):
        rows = slice(c0, c0 + ROW_CHUNK)
        for h in range(MEM_HEADS):
            hc = slice(h * N_MEM, (h + 1) * N_MEM)
            s = sbuf[rows, hc]
            e = jnp.exp(s - jnp.max(s, axis=-1, keepdims=True))
            pbuf[rows, hc] = (e * (1.0 / jnp.sum(e, axis=-1, keepdims=True))).astype(BF16)
    for a0 in range(0, m, att_rows):
        rows = slice(a0, a0 + att_rows)
        ubuf[rows, _cols(COL_X_Q)] = jnp.dot(pbuf[rows, :], vbd[a0 // tt], preferred_element_type=F32)

    for b0 in range(0, m, BLOCK_ROWS):
        blk = slice(b0, b0 + BLOCK_ROWS)
        chunks = [(c0,) + divmod(c0, tt) for c0 in range(b0, b0 + BLOCK_ROWS, ROW_CHUNK)]

        for c0, r, q0 in chunks:
            rows = slice(c0, c0 + ROW_CHUNK)
            v = ubuf[rows, _cols(COL_A_C)] * ubuf[rows, _cols(COL_A_X)]
            glu = ubuf[rows, _cols(COL_B_V)] * _sigmoid(ubuf[rows, _cols(COL_B_A)])
            cu = ubuf[rows, _cols(COL_C_U)]
            for c in range(SLABS):
                lanes = slice(c * LANES, (c + 1) * LANES)
                aext[r, c, PAD_A + q0:PAD_A + q0 + ROW_CHUNK, :] = v[:, lanes]
                bext[r, c, PAD_B + q0:PAD_B + q0 + ROW_CHUNK, :] = glu[:, lanes]
                pext[r, c, PAD_P + q0:PAD_P + q0 + ROW_CHUNK, :] = cu[:, lanes]

        for out_row, r, q0 in chunks:
            for c in range(SLABS):
                lanes = slice(c * LANES, (c + 1) * LANES)

                acc = [None] * GATHER_STRIDE
                for k in range(CONV_A_W):
                    w = wa[k, :, lanes]
                    for j in steps:
                        term = w * _gather(aext.at[r, c], PAD_A - HIST_A + k + q0 + j)
                        acc[j] = term if k == 0 else acc[j] + term
                for j in steps:
                    mixed[0, c, pl.ds(out_row + j, SUBLANES, stride=GATHER_STRIDE), :] = acc[j]

                acc = [None] * GATHER_STRIDE
                for k in range(CONV_B_W):
                    w = wb[k, :, lanes]
                    for j in steps:
                        term = w * _gather(bext.at[r, c], PAD_B - HIST_B + k + q0 + j)
                        acc[j] = term if k == 0 else acc[j] + term
                for j in steps:
                    mixed[1, c, pl.ds(out_row + j, SUBLANES, stride=GATHER_STRIDE), :] = acc[j] + cbb_ref[:, lanes]

                w_near, w_far = POOL_WINDOWS[2 * c], POOL_WINDOWS[2 * c + 1]
                for j in steps:
                    cu = _gather(pext.at[r, c], PAD_P + q0 + j)
                    near = cu
                    for d in range(1, w_near):
                        near = near + _gather(pext.at[r, c], PAD_P + q0 + j - d)
                    far = near
                    for d in range(w_near, w_far):
                        far = far + _gather(pext.at[r, c], PAD_P + q0 + j - d)
                    pos = sub_pos + (q0 + j)
                    inv_near = 1.0 / jnp.minimum(w_near, pos + 1).astype(F32)
                    inv_far = 1.0 / jnp.minimum(w_far, pos + 1).astype(F32)
                    pooled = jnp.where(low_half, near * inv_near, far * inv_far)
                    mixed[2, c, pl.ds(out_row + j, SUBLANES, stride=GATHER_STRIDE), :] = pooled - cu

        for c0, _, _ in chunks:
            rows = slice(c0, c0 + ROW_CHUNK)

            def mixed_rows(g):
                return jnp.concatenate([mixed[g, c, rows, :] for c in range(SLABS)], axis=-1)

            y_a = ubuf[rows, _cols(COL_A_B)] * mixed_rows(0) * _silu(ubuf[rows, _cols(COL_A_G)])
            ybuf[rows, _cols(0)] = y_a.astype(BF16)

            z = mixed_rows(1)
            mu = jnp.mean(z, axis=-1, keepdims=True)
            zc = z - mu
            var = jnp.mean(zc * zc, axis=-1, keepdims=True)
            zn = zc * lax.rsqrt(var + EPS) * lng_ref[...] + lnb_ref[...]
            y_b = _silu(zn) * _silu(ubuf[rows, _cols(COL_B_G)])
            ybuf[rows, _cols(1)] = y_b.astype(BF16)

            dbuf[rows, :] = mixed_rows(2).astype(BF16)

        ubuf[blk, _cols(COL_C_U)] = jnp.dot(dbuf[blk, :], pw_ref[...], preferred_element_type=F32)

        for c0, _, _ in chunks:
            rows = slice(c0, c0 + ROW_CHUNK)
            y_c = (ubuf[rows, _cols(COL_C_U)] + pb_ref[...]) * ps_ref[...] * _silu(ubuf[rows, _cols(COL_C_G)])
            ybuf[rows, _cols(2)] = y_c.astype(BF16)
            y_x = ubuf[rows, _cols(COL_X_Q)] * _silu(ubuf[rows, _cols(COL_X_G)])
            ybuf[rows, _cols(3)] = y_x.astype(BF16)

        ubuf[blk, 0:D_MODEL] = jnp.dot(ybuf[blk, :], wout_ref[...], preferred_element_type=F32)
        for c0, r, q0 in chunks:
            out = x_ref[r, q0:q0 + ROW_CHUNK, :] + ubuf[c0:c0 + ROW_CHUNK, 0:D_MODEL]
            if final_norm:
                ms = jnp.mean(out * out, axis=-1, keepdims=True)
                out = out * lax.rsqrt(ms + EPS) * fg_ref[...]
            y_ref[r, q0:q0 + ROW_CHUNK, :] = out

    for ext, _, new_ref, pad, hn in hist:
        for r in range(nb):
            for c in range(SLABS):
                tail = ext[r, c, pad + tt - hn:pad + tt, :]
                ext[r, c, pad - hn:pad, :] = tail

                @pl.when(t == nt - 1)
                def _():
                    new_ref[r, :, c * LANES:(c + 1) * LANES] = tail


def _layer_call(layer, x, states, mem_k, mem_v, params, *, tile_rows, nb, pos0, final_norm):
    b, t_len, _ = x.shape
    tt = tile_rows // nb
    grid = (b // nb, t_len // tt)
    m = tile_rows
    has_state = states is not None

    def layer_block(a):
        index_map = lambda i, j: (layer,) + (0,) * (a.ndim - 1)
        if a.size // a.shape[0] >= D_MODEL * D_MODEL:
            return pl.BlockSpec((None,) + a.shape[1:], index_map, pipeline_mode=pl.Buffered(1))
        return pl.BlockSpec((None,) + a.shape[1:], index_map)

    def state_block(hn):
        return pl.BlockSpec((None, nb, hn, GROUP_W), lambda i, j: (layer, i, 0, 0))

    def new_state_block(hn):
        return pl.BlockSpec((nb, hn, GROUP_W), lambda i, j: (i, 0, 0))

    x_spec = pl.BlockSpec((nb, tt, D_MODEL), lambda i, j: (i, j, 0))
    kv_spec = pl.BlockSpec((None, nb, N_MEM, GROUP_W), lambda i, j: (layer, i, 0, 0))
    hists = (HIST_A, HIST_B, HIST_P)
    inputs, in_specs = [x], [x_spec]
    if has_state:
        inputs += list(states)
        in_specs += [state_block(hn) for hn in hists]
    inputs += [mem_k, mem_v]
    in_specs += [kv_spec, kv_spec]
    inputs += list(params)
    in_specs += [layer_block(p) for p in params[:-1]] + [pl.BlockSpec(params[-1].shape, lambda i, j: (0, 0))]

    out_specs = [x_spec] + [new_state_block(hn) for hn in hists]
    out_shape = [jax.ShapeDtypeStruct(x.shape, F32)] + [jax.ShapeDtypeStruct((b, hn, GROUP_W), F32) for hn in hists]
    scratch = [pltpu.VMEM((nb, SLABS, pad + tt, LANES), F32) for pad in (PAD_A, PAD_B, PAD_P)]
    scratch += [pltpu.VMEM((CONV_A_W, SUBLANES, GROUP_W), F32),
                pltpu.VMEM((CONV_B_W, SUBLANES, GROUP_W), F32),
                pltpu.VMEM((D_MODEL, IN_COLS), BF16),
                pltpu.VMEM((3, SLABS, m, LANES), F32),
                pltpu.VMEM((m, D_MODEL), BF16),
                pltpu.VMEM((m, IN_COLS), F32),
                pltpu.VMEM((m, GROUP_W), BF16),
                pltpu.VMEM((m, MEM_HEADS * N_MEM), F32),
                pltpu.VMEM((m, MEM_HEADS * N_MEM), BF16),
                pltpu.VMEM((m, D_MODEL), BF16),
                pltpu.VMEM((nb, GROUP_W, MEM_HEADS * N_MEM), BF16),
                pltpu.VMEM((nb, MEM_HEADS * N_MEM, GROUP_W), BF16)]
    body = functools.partial(_layer_kernel, nb=nb, tt=tt, pos0=pos0, has_state=has_state, final_norm=final_norm)
    return pl.pallas_call(
        body, grid=grid, in_specs=in_specs, out_specs=out_specs, out_shape=out_shape, scratch_shapes=scratch,
        compiler_params=pltpu.CompilerParams(dimension_semantics=("arbitrary", "arbitrary"),
                                             vmem_limit_bytes=VMEM_LIMIT_BYTES),
        name="encoder_layer",
    )(*inputs)


def _memkv_kernel(mem_ref, g_ref, wk_ref, wv_ref, k_ref, v_ref):
    x = mem_ref[...]
    ms = jnp.mean(x * x, axis=-1, keepdims=True)
    mn = (x * lax.rsqrt(ms + EPS) * g_ref[...]).astype(BF16)
    k_ref[...] = jnp.dot(mn, wk_ref[...], preferred_element_type=F32)
    v_ref[...] = jnp.dot(mn, wv_ref[...], preferred_element_type=F32)


def _memkv_call(mem2d, g, wk, wv):
    depth = g.shape[0]
    rows = mem2d.shape[0]
    out = jax.ShapeDtypeStruct((depth, rows, GROUP_W), F32)
    return pl.pallas_call(
        _memkv_kernel, grid=(depth,),
        in_specs=[pl.BlockSpec((rows, D_MODEL), lambda l: (0, 0)),
                  pl.BlockSpec((None, 1, D_MODEL), lambda l: (l, 0, 0)),
                  pl.BlockSpec((None, D_MODEL, GROUP_W), lambda l: (l, 0, 0)),
                  pl.BlockSpec((None, D_MODEL, GROUP_W), lambda l: (l, 0, 0))],
        out_specs=[pl.BlockSpec((None, rows, GROUP_W), lambda l: (l, 0, 0)),
                   pl.BlockSpec((None, rows, GROUP_W), lambda l: (l, 0, 0))],
        out_shape=[out, out],
        compiler_params=pltpu.CompilerParams(dimension_semantics=("arbitrary",), vmem_limit_bytes=VMEM_LIMIT_BYTES),
        name="memory_kv",
    )(mem2d, g, wk, wv)


def _pool_block_diag(pool_w):
    depth, g = pool_w.shape[:2]
    eye = jnp.eye(g, dtype=pool_w.dtype)
    return jnp.einsum('lgce,gh->lgche', pool_w, eye).reshape(depth, g * POOL_CH, g * POOL_CH)


def kernel(x_prompt, x_sample, mem_prompt, state_conv_a, state_conv_b, state_pool, cache_mem_k, cache_mem_v,
           norm_g, w_in, conv_a_w, conv_b_w, conv_b_bias, ln_b_g, ln_b_b, pool_w, pool_bias, pool_scale,
           mem_norm_g, w_mem_k, w_mem_v, w_out, final_norm_g):
    depth = norm_g.shape[0]
    bp, seq, _ = x_prompt.shape
    bs, dec_seq, _ = x_sample.shape
    sample_nb = SAMPLE_TILE_ROWS // dec_seq
    assert seq % PROMPT_TILE_ROWS == 0 and SAMPLE_TILE_ROWS % dec_seq == 0 and bs % sample_nb == 0
    assert dec_seq % ROW_CHUNK == 0 and BLOCK_ROWS % dec_seq == 0
    assert PROMPT_TILE_ROWS % max(BLOCK_ROWS, ATT_ROWS) == 0 and SAMPLE_TILE_ROWS % BLOCK_ROWS == 0

    mk_all, mv_all = _memkv_call(mem_prompt.reshape(bp * N_MEM, D_MODEL), mem_norm_g[:, None, :],
                                 w_mem_k.astype(BF16), w_mem_v.astype(BF16))
    mk4 = mk_all.reshape(depth, bp, N_MEM, GROUP_W)
    mv4 = mv_all.reshape(depth, bp, N_MEM, GROUP_W)
    cache_k = cache_mem_k.reshape(depth, bs, N_MEM, GROUP_W)
    cache_v = cache_mem_v.reshape(depth, bs, N_MEM, GROUP_W)

    def row(a):
        return a[:, None, :]

    params = (row(norm_g), w_in.astype(BF16), conv_a_w, conv_b_w, row(conv_b_bias), row(ln_b_g), row(ln_b_b),
              _pool_block_diag(pool_w).astype(BF16), row(pool_bias), row(pool_scale), w_out.astype(BF16),
              final_norm_g[None, :])
    states = (state_conv_a, state_conv_b, state_pool)

    xp, xs = x_prompt, x_sample
    new_p, new_s = [], []
    for l in range(depth):
        last = l == depth - 1
        xp, *st = _layer_call(l, xp, None, mk4, mv4, params, tile_rows=PROMPT_TILE_ROWS, nb=1, pos0=0,
                              final_norm=last)
        new_p.append(st)
        xs, *st = _layer_call(l, xs, states, cache_k, cache_v, params, tile_rows=SAMPLE_TILE_ROWS, nb=sample_nb,
                              pos0=PAST_LEN, final_norm=last)
        new_s.append(st)

    def stack(new, i):
        return jnp.stack([st[i] for st in new])

    mk_out = mk_all.reshape(depth, bp, N_MEM, MEM_HEADS, MEM_HEAD_DIM)
    mv_out = mv_all.reshape(depth, bp, N_MEM, MEM_HEADS, MEM_HEAD_DIM)
    return (xp, xs, stack(new_p, 0), stack(new_p, 1), stack(new_p, 2), mk_out, mv_out,
            stack(new_s, 0), stack(new_s, 1), stack(new_s, 2))
```

```python
import functools

import jax
import jax.numpy as jnp
from jax import lax
from jax.experimental import pallas as pl
from jax.experimental.pallas import tpu as pltpu

D_MODEL = 1024
GROUP_W = 256
IN_COLS = 11 * GROUP_W
CONV_A_W = 3
CONV_B_W = 31
POOL_WINDOWS = (2, 4, 8, 16)
POOL_CH = 64
HIST_A = CONV_A_W - 1
HIST_B = CONV_B_W - 1
HIST_P = max(POOL_WINDOWS) - 1
N_MEM = 256
MEM_HEADS = 4
MEM_HEAD_DIM = 64
EPS = 1e-6
PAST_LEN = 1024

PROJ_PERM = (4, 5, 1, 2, 7, 9, 0, 3, 6, 8, 10)
COL_B_V, COL_B_A, COL_A_C, COL_A_X, COL_C_U, COL_X_Q, COL_A_B, COL_A_G, COL_B_G, COL_C_G, COL_X_G = range(11)

LANES = 128
SUBLANES = 8
SLABS = GROUP_W // LANES
PAD_A = 8
PAD_B = 32
PAD_P = 16
ROW_CHUNK = 32
GATHER_STRIDE = ROW_CHUNK // SUBLANES
PROMPT_TILE_ROWS = 1024
SAMPLE_TILE_ROWS = 512
BLOCK_ROWS = 512
ATT_ROWS = 512
VMEM_LIMIT_BYTES = 56 * 1024 * 1024

F32 = jnp.float32
BF16 = jnp.bfloat16


def _cols(g):
    return slice(g * GROUP_W, (g + 1) * GROUP_W)


def _sigmoid(x):
    return 1.0 / (1.0 + jnp.exp(-x))


def _silu(x):
    return x * _sigmoid(x)


def _gather(slab, base):
    return slab[pl.ds(base, SUBLANES, stride=GATHER_STRIDE), :]


def _layer_kernel(*refs, nb, tt, pos0, has_state, final_norm):
    if has_state:
        x_ref, sa_ref, sb_ref, sp_ref, k_ref, v_ref = refs[:6]
        rest = refs[6:]
    else:
        x_ref, k_ref, v_ref = refs[:3]
        sa_ref = sb_ref = sp_ref = None
        rest = refs[3:]
    (ng_ref, win_ref, caw_ref, cbw_ref, cbb_ref, lng_ref, lnb_ref, pw_ref, pb_ref, ps_ref, wout_ref, fg_ref,
     y_ref, na_ref, nb_ref, np_ref,
     aext, bext, pext, wa, wb, wperm, mixed, hbuf, ubuf, dbuf, sbuf, pbuf, ybuf, kbd, vbd) = rest
    t = pl.program_id(1)
    nt = pl.num_programs(1)
    m = nb * tt
    hist = ((aext, sa_ref, na_ref, PAD_A, HIST_A), (bext, sb_ref, nb_ref, PAD_B, HIST_B),
            (pext, sp_ref, np_ref, PAD_P, HIST_P))

    @pl.when((pl.program_id(0) == 0) & (t == 0))
    def _():
        for j, g in enumerate(PROJ_PERM):
            wperm[:, _cols(j)] = win_ref[:, _cols(g)]

    @pl.when(t == 0)
    def _():
        for ext, st_ref, _, pad, hn in hist:
            for r in range(nb):
                for c in range(SLABS):
                    lanes = slice(c * LANES, (c + 1) * LANES)
                    ext[r, c, pad - hn:pad, :] = (st_ref[r, :, lanes] if has_state else jnp.zeros((hn, LANES), F32))
        for k in range(CONV_A_W):
            wa[k] = jnp.broadcast_to(caw_ref[k:k + 1, :], (SUBLANES, GROUP_W))
        for k in range(CONV_B_W):
            wb[k] = jnp.broadcast_to(cbw_ref[k:k + 1, :], (SUBLANES, GROUP_W))
        row_head = lax.broadcasted_iota(jnp.int32, (GROUP_W, N_MEM), 0) // MEM_HEAD_DIM
        lane_head = lax.broadcasted_iota(jnp.int32, (N_MEM, GROUP_W), 1) // MEM_HEAD_DIM
        for r in range(nb):
            k_t = k_ref[r].T
            v_r = v_ref[r]
            for h in range(MEM_HEADS):
                kbd[r, :, h * N_MEM:(h + 1) * N_MEM] = jnp.where(row_head == h, k_t, 0.0).astype(BF16)
                vbd[r, h * N_MEM:(h + 1) * N_MEM, :] = jnp.where(lane_head == h, v_r, 0.0).astype(BF16)

    for c0 in range(0, m, ROW_CHUNK):
        r, q0 = divmod(c0, tt)
        x = x_ref[r, q0:q0 + ROW_CHUNK, :]
        ms = jnp.mean(x * x, axis=-1, keepdims=True)
        hbuf[c0:c0 + ROW_CHUNK, :] = (x * lax.rsqrt(ms + EPS) * ng_ref[...]).astype(BF16)

    ubuf[...] = jnp.dot(hbuf[...], wperm[...], preferred_element_type=F32)

    sub_pos = pos0 + t * tt + GATHER_STRIDE * lax.broadcasted_iota(jnp.int32, (SUBLANES, 1), 0)
    low_half = lax.broadcasted_iota(jnp.int32, (SUBLANES, LANES), 1) < POOL_CH
    steps = range(GATHER_STRIDE)

    att_rows = min(tt, ATT_ROWS)
    for a0 in range(0, m, att_rows):
        rows = slice(a0, a0 + att_rows)
        qb = (ubuf[rows, _cols(COL_X_Q)] * (MEM_HEAD_DIM ** -0.5)).astype(BF16)
        sbuf[rows, :] = jnp.dot(qb, kbd[a0 // tt], preferred_element_type=F32)
    for a0 in range(0, m, att_rows):
        for c0 in range(a0, a0 + att_rows, ROW_CHUNK):
            rows = slice(c0, c0 + ROW_CHUNK)
            for h in range(MEM_HEADS):
                hc = slice(h * N_MEM, (h + 1) * N_MEM)
                s = sbuf[rows, hc]
                e = jnp.exp(s - jnp.max(s, axis=-1, keepdims=True))
                pbuf[rows, hc] = (e * (1.0 / jnp.sum(e, axis=-1, keepdims=True))).astype(BF16)
        rows = slice(a0, a0 + att_rows)
        ubuf[rows, _cols(COL_X_Q)] = jnp.dot(pbuf[rows, :], vbd[a0 // tt], preferred_element_type=F32)

    for b0 in range(0, m, BLOCK_ROWS):
        blk = slice(b0, b0 + BLOCK_ROWS)
        chunks = [(c0,) + divmod(c0, tt) for c0 in range(b0, b0 + BLOCK_ROWS, ROW_CHUNK)]

        for c0, r, q0 in chunks:
            rows = slice(c0, c0 + ROW_CHUNK)
            v = ubuf[rows, _cols(COL_A_C)] * ubuf[rows, _cols(COL_A_X)]
            glu = ubuf[rows, _cols(COL_B_V)] * _sigmoid(ubuf[rows, _cols(COL_B_A)])
            cu = ubuf[rows, _cols(COL_C_U)]
            for c in range(SLABS):
                lanes = slice(c * LANES, (c + 1) * LANES)
                aext[r, c, PAD_A + q0:PAD_A + q0 + ROW_CHUNK, :] = v[:, lanes]
                bext[r, c, PAD_B + q0:PAD_B + q0 + ROW_CHUNK, :] = glu[:, lanes]
                pext[r, c, PAD_P + q0:PAD_P + q0 + ROW_CHUNK, :] = cu[:, lanes]

        for out_row, r, q0 in chunks:
            for c in range(SLABS):
                lanes = slice(c * LANES, (c + 1) * LANES)

                acc = [None] * GATHER_STRIDE
                for k in range(CONV_A_W):
                    w = wa[k, :, lanes]
                    for j in steps:
                        term = w * _gather(aext.at[r, c], PAD_A - HIST_A + k + q0 + j)
                        acc[j] = term if k == 0 else acc[j] + term
                for j in steps:
                    mixed[0, c, pl.ds(out_row + j, SUBLANES, stride=GATHER_STRIDE), :] = acc[j]

                acc = [None] * GATHER_STRIDE
                for k in range(CONV_B_W):
                    w = wb[k, :, lanes]
                    for j in steps:
                        term = w * _gather(bext.at[r, c], PAD_B - HIST_B + k + q0 + j)
                        acc[j] = term if k == 0 else acc[j] + term
                for j in steps:
                    mixed[1, c, pl.ds(out_row + j, SUBLANES, stride=GATHER_STRIDE), :] = acc[j] + cbb_ref[:, lanes]

                w_near, w_far = POOL_WINDOWS[2 * c], POOL_WINDOWS[2 * c + 1]
                for j in steps:
                    cu = _gather(pext.at[r, c], PAD_P + q0 + j)
                    near = cu
                    for d in range(1, w_near):
                        near = near + _gather(pext.at[r, c], PAD_P + q0 + j - d)
                    far = near
                    for d in range(w_near, w_far):
                        far = far + _gather(pext.at[r, c], PAD_P + q0 + j - d)
                    pos = sub_pos + (q0 + j)
                    inv_near = 1.0 / jnp.minimum(w_near, pos + 1).astype(F32)
                    inv_far = 1.0 / jnp.minimum(w_far, pos + 1).astype(F32)
                    pooled = jnp.where(low_half, near * inv_near, far * inv_far)
                    mixed[2, c, pl.ds(out_row + j, SUBLANES, stride=GATHER_STRIDE), :] = pooled - cu

        for c0, _, _ in chunks:
            rows = slice(c0, c0 + ROW_CHUNK)

            def mixed_rows(g):
                return jnp.concatenate([mixed[g, c, rows, :] for c in range(SLABS)], axis=-1)

            y_a = ubuf[rows, _cols(COL_A_B)] * mixed_rows(0) * _silu(ubuf[rows, _cols(COL_A_G)])
            ybuf[rows, _cols(0)] = y_a.astype(BF16)

            z = mixed_rows(1)
            mu = jnp.mean(z, axis=-1, keepdims=True)
            zc = z - mu
            var = jnp.mean(zc * zc, axis=-1, keepdims=True)
            zn = zc * lax.rsqrt(var + EPS) * lng_ref[...] + lnb_ref[...]
            y_b = _silu(zn) * _silu(ubuf[rows, _cols(COL_B_G)])
            ybuf[rows, _cols(1)] = y_b.astype(BF16)

            dbuf[rows, :] = mixed_rows(2).astype(BF16)

        ubuf[blk, _cols(COL_C_U)] = jnp.dot(dbuf[blk, :], pw_ref[...], preferred_element_type=F32)

        for c0, _, _ in chunks:
            rows = slice(c0, c0 + ROW_CHUNK)
            y_c = (ubuf[rows, _cols(COL_C_U)] + pb_ref[...]) * ps_ref[...] * _silu(ubuf[rows, _cols(COL_C_G)])
            ybuf[rows, _cols(2)] = y_c.astype(BF16)
            y_x = ubuf[rows, _cols(COL_X_Q)] * _silu(ubuf[rows, _cols(COL_X_G)])
            ybuf[rows, _cols(3)] = y_x.astype(BF16)

        ubuf[blk, 0:D_MODEL] = jnp.dot(ybuf[blk, :], wout_ref[...], preferred_element_type=F32)
        for c0, r, q0 in chunks:
            out = x_ref[r, q0:q0 + ROW_CHUNK, :] + ubuf[c0:c0 + ROW_CHUNK, 0:D_MODEL]
            if final_norm:
                ms = jnp.mean(out * out, axis=-1, keepdims=True)
                out = out * lax.rsqrt(ms + EPS) * fg_ref[...]
            y_ref[r, q0:q0 + ROW_CHUNK, :] = out

    for ext, _, new_ref, pad, hn in hist:
        for r in range(nb):
            for c in range(SLABS):
                tail = ext[r, c, pad + tt - hn:pad + tt, :]
                ext[r, c, pad - hn:pad, :] = tail

                @pl.when(t == nt - 1)
                def _():
                    new_ref[r, :, c * LANES:(c + 1) * LANES] = tail


def _layer_call(layer, x, states, mem_k, mem_v, params, *, tile_rows, nb, pos0, final_norm):
    b, t_len, _ = x.shape
    tt = tile_rows // nb
    grid = (b // nb, t_len // tt)
    m = tile_rows
    has_state = states is not None

    def layer_block(a):
        index_map = lambda i, j: (layer,) + (0,) * (a.ndim - 1)
        if a.size // a.shape[0] >= D_MODEL * D_MODEL:
            return pl.BlockSpec((None,) + a.shape[1:], index_map, pipeline_mode=pl.Buffered(1))
        return pl.BlockSpec((None,) + a.shape[1:], index_map)

    def state_block(hn):
        return pl.BlockSpec((None, nb, hn, GROUP_W), lambda i, j: (layer, i, 0, 0))

    def new_state_block(hn):
        return pl.BlockSpec((nb, hn, GROUP_W), lambda i, j: (i, 0, 0))

    x_spec = pl.BlockSpec((nb, tt, D_MODEL), lambda i, j: (i, j, 0))
    kv_spec = pl.BlockSpec((None, nb, N_MEM, GROUP_W), lambda i, j: (layer, i, 0, 0))
    hists = (HIST_A, HIST_B, HIST_P)
    inputs, in_specs = [x], [x_spec]
    if has_state:
        inputs += list(states)
        in_specs += [state_block(hn) for hn in hists]
    inputs += [mem_k, mem_v]
    in_specs += [kv_spec, kv_spec]
    inputs += list(params)
    in_specs += [layer_block(p) for p in params[:-1]] + [pl.BlockSpec(params[-1].shape, lambda i, j: (0, 0))]

    out_specs = [x_spec] + [new_state_block(hn) for hn in hists]
    out_shape = [jax.ShapeDtypeStruct(x.shape, F32)] + [jax.ShapeDtypeStruct((b, hn, GROUP_W), F32) for hn in hists]
    scratch = [pltpu.VMEM((nb, SLABS, pad + tt, LANES), F32) for pad in (PAD_A, PAD_B, PAD_P)]
    scratch += [pltpu.VMEM((CONV_A_W, SUBLANES, GROUP_W), F32),
                pltpu.VMEM((CONV_B_W, SUBLANES, GROUP_W), F32),
                pltpu.VMEM((D_MODEL, IN_COLS), BF16),
                pltpu.VMEM((3, SLABS, m, LANES), F32),
                pltpu.VMEM((m, D_MODEL), BF16),
                pltpu.VMEM((m, IN_COLS), F32),
                pltpu.VMEM((m, GROUP_W), BF16),
                pltpu.VMEM((m, MEM_HEADS * N_MEM), F32),
                pltpu.VMEM((m, MEM_HEADS * N_MEM), BF16),
                pltpu.VMEM((m, D_MODEL), BF16),
                pltpu.VMEM((nb, GROUP_W, MEM_HEADS * N_MEM), BF16),
                pltpu.VMEM((nb, MEM_HEADS * N_MEM, GROUP_W), BF16)]
    body = functools.partial(_layer_kernel, nb=nb, tt=tt, pos0=pos0, has_state=has_state, final_norm=final_norm)
    return pl.pallas_call(
        body, grid=grid, in_specs=in_specs, out_specs=out_specs, out_shape=out_shape, scratch_shapes=scratch,
        compiler_params=pltpu.CompilerParams(dimension_semantics=("arbitrary", "arbitrary"),
                                             vmem_limit_bytes=VMEM_LIMIT_BYTES),
        name="encoder_layer",
    )(*inputs)


def _memkv_kernel(mem_ref, g_ref, wk_ref, wv_ref, k_ref, v_ref):
    x = mem_ref[...]
    ms = jnp.mean(x * x, axis=-1, keepdims=True)
    mn = (x * lax.rsqrt(ms + EPS) * g_ref[...]).astype(BF16)
    k_ref[...] = jnp.dot(mn, wk_ref[...], preferred_element_type=F32)
    v_ref[...] = jnp.dot(mn, wv_ref[...], preferred_element_type=F32)


def _memkv_call(mem2d, g, wk, wv):
    depth = g.shape[0]
    rows = mem2d.shape[0]
    out = jax.ShapeDtypeStruct((depth, rows, GROUP_W), F32)
    return pl.pallas_call(
        _memkv_kernel, grid=(depth,),
        in_specs=[pl.BlockSpec((rows, D_MODEL), lambda l: (0, 0)),
                  pl.BlockSpec((None, 1, D_MODEL), lambda l: (l, 0, 0)),
                  pl.BlockSpec((None, D_MODEL, GROUP_W), lambda l: (l, 0, 0)),
                  pl.BlockSpec((None, D_MODEL, GROUP_W), lambda l: (l, 0, 0))],
        out_specs=[pl.BlockSpec((None, rows, GROUP_W), lambda l: (l, 0, 0)),
                   pl.BlockSpec((None, rows, GROUP_W), lambda l: (l, 0, 0))],
        out_shape=[out, out],
        compiler_params=pltpu.CompilerParams(dimension_semantics=("arbitrary",), vmem_limit_bytes=VMEM_LIMIT_BYTES),
        name="memory_kv",
    )(mem2d, g, wk, wv)


def _pool_block_diag(pool_w):
    depth, g = pool_w.shape[:2]
    eye = jnp.eye(g, dtype=pool_w.dtype)
    return jnp.einsum('lgce,gh->lgche', pool_w, eye).reshape(depth, g * POOL_CH, g * POOL_CH)


def kernel(x_prompt, x_sample, mem_prompt, state_conv_a, state_conv_b, state_pool, cache_mem_k, cache_mem_v,
           norm_g, w_in, conv_a_w, conv_b_w, conv_b_bias, ln_b_g, ln_b_b, pool_w, pool_bias, pool_scale,
           mem_norm_g, w_mem_k, w_mem_v, w_out, final_norm_g):
    depth = norm_g.shape[0]
    bp, seq, _ = x_prompt.shape
    bs, dec_seq, _ = x_sample.shape
    sample_nb = SAMPLE_TILE_ROWS // dec_seq
    assert seq % PROMPT_TILE_ROWS == 0 and SAMPLE_TILE_ROWS % dec_seq == 0 and bs % sample_nb == 0
    assert dec_seq % ROW_CHUNK == 0 and BLOCK_ROWS % dec_seq == 0
    assert PROMPT_TILE_ROWS % max(BLOCK_ROWS, ATT_ROWS) == 0 and SAMPLE_TILE_ROWS % BLOCK_ROWS == 0

    mk_all, mv_all = _memkv_call(mem_prompt.reshape(bp * N_MEM, D_MODEL), mem_norm_g[:, None, :],
                                 w_mem_k.astype(BF16), w_mem_v.astype(BF16))
    mk4 = mk_all.reshape(depth, bp, N_MEM, GROUP_W)
    mv4 = mv_all.reshape(depth, bp, N_MEM, GROUP_W)
    cache_k = cache_mem_k.reshape(depth, bs, N_MEM, GROUP_W)
    cache_v = cache_mem_v.reshape(depth, bs, N_MEM, GROUP_W)

    def row(a):
        return a[:, None, :]

    params = (row(norm_g), w_in.astype(BF16), conv_a_w, conv_b_w, row(conv_b_bias), row(ln_b_g), row(ln_b_b),
              _pool_block_diag(pool_w).astype(BF16), row(pool_bias), row(pool_scale), w_out.astype(BF16),
              final_norm_g[None, :])
    states = (state_conv_a, state_conv_b, state_pool)

    xp, xs = x_prompt, x_sample
    new_p, new_s = [], []
    for l in range(depth):
        last = l == depth - 1
        xp, *st = _layer_call(l, xp, None, mk4, mv4, params, tile_rows=PROMPT_TILE_ROWS, nb=1, pos0=0,
                              final_norm=last)
        new_p.append(st)
        xs, *st = _layer_call(l, xs, states, cache_k, cache_v, params, tile_rows=SAMPLE_TILE_ROWS, nb=sample_nb,
                              pos0=PAST_LEN, final_norm=last)
        new_s.append(st)

    def stack(new, i):
        return jnp.stack([st[i] for st in new])

    mk_out = mk_all.reshape(depth, bp, N_MEM, MEM_HEADS, MEM_HEAD_DIM)
    mv_out = mv_all.reshape(depth, bp, N_MEM, MEM_HEADS, MEM_HEAD_DIM)
    return (xp, xs, stack(new_p, 0), stack(new_p, 1), stack(new_p, 2), mk_out, mv_out,
            stack(new_s, 0), stack(new_s, 1), stack(new_s, 2))
```

```python
import functools

import jax
import jax.numpy as jnp
from jax import lax
from jax.experimental import pallas as pl
from jax.experimental.pallas import tpu as pltpu

D_MODEL = 1024
GROUP_W = 256
IN_COLS = 11 * GROUP_W
CONV_A_W = 3
CONV_B_W = 31
POOL_WINDOWS = (2, 4, 8, 16)
POOL_CH = 64
HIST_A = CONV_A_W - 1
HIST_B = CONV_B_W - 1
HIST_P = max(POOL_WINDOWS) - 1
N_MEM = 256
MEM_HEADS = 4
MEM_HEAD_DIM = 64
EPS = 1e-6
PAST_LEN = 1024

PROJ_PERM = (4, 5, 1, 2, 7, 9, 0, 3, 6, 8, 10)
COL_B_V, COL_B_A, COL_A_C, COL_A_X, COL_C_U, COL_X_Q, COL_A_B, COL_A_G, COL_B_G, COL_C_G, COL_X_G = range(11)

LANES = 128
SUBLANES = 8
SLABS = GROUP_W // LANES
PAD_A = 8
PAD_B = 32
PAD_P = 16
ROW_CHUNK = 32
GATHER_STRIDE = ROW_CHUNK // SUBLANES
PROMPT_TILE_ROWS = 1024
SAMPLE_TILE_ROWS = 512
BLOCK_ROWS = 1024
ATT_ROWS = 512
VMEM_LIMIT_BYTES = 56 * 1024 * 1024

F32 = jnp.float32
BF16 = jnp.bfloat16


def _cols(g):
    return slice(g * GROUP_W, (g + 1) * GROUP_W)


def _sigmoid(x):
    return 1.0 / (1.0 + jnp.exp(-x))


def _silu(x):
    return x * _sigmoid(x)


def _gather(slab, base):
    return slab[pl.ds(base, SUBLANES, stride=GATHER_STRIDE), :]


def _layer_kernel(*refs, nb, tt, pos0, has_state, final_norm):
    if has_state:
        x_ref, sa_ref, sb_ref, sp_ref, k_ref, v_ref = refs[:6]
        rest = refs[6:]
    else:
        x_ref, k_ref, v_ref = refs[:3]
        sa_ref = sb_ref = sp_ref = None
        rest = refs[3:]
    (ng_ref, win_ref, caw_ref, cbw_ref, cbb_ref, lng_ref, lnb_ref, pw_ref, pb_ref, ps_ref, wout_ref, fg_ref,
     y_ref, na_ref, nb_ref, np_ref,
     aext, bext, pext, wa, wb, wperm, mixed, hbuf, ubuf, dbuf, sbuf, pbuf, ybuf, kbd, vbd) = rest
    t = pl.program_id(1)
    nt = pl.num_programs(1)
    m = nb * tt
    hist = ((aext, sa_ref, na_ref, PAD_A, HIST_A), (bext, sb_ref, nb_ref, PAD_B, HIST_B),
            (pext, sp_ref, np_ref, PAD_P, HIST_P))

    @pl.when((pl.program_id(0) == 0) & (t == 0))
    def _():
        for j, g in enumerate(PROJ_PERM):
            wperm[:, _cols(j)] = win_ref[:, _cols(g)]

    @pl.when(t == 0)
    def _():
        for ext, st_ref, _, pad, hn in hist:
            for r in range(nb):
                for c in range(SLABS):
                    lanes = slice(c * LANES, (c + 1) * LANES)
                    ext[r, c, pad - hn:pad, :] = (st_ref[r, :, lanes] if has_state else jnp.zeros((hn, LANES), F32))
        for k in range(CONV_A_W):
            wa[k] = jnp.broadcast_to(caw_ref[k:k + 1, :], (SUBLANES, GROUP_W))
        for k in range(CONV_B_W):
            wb[k] = jnp.broadcast_to(cbw_ref[k:k + 1, :], (SUBLANES, GROUP_W))
        row_head = lax.broadcasted_iota(jnp.int32, (GROUP_W, N_MEM), 0) // MEM_HEAD_DIM
        lane_head = lax.broadcasted_iota(jnp.int32, (N_MEM, GROUP_W), 1) // MEM_HEAD_DIM
        for r in range(nb):
            k_t = k_ref[r].T
            v_r = v_ref[r]
            for h in range(MEM_HEADS):
                kbd[r, :, h * N_MEM:(h + 1) * N_MEM] = jnp.where(row_head == h, k_t, 0.0).astype(BF16)
                vbd[r, h * N_MEM:(h + 1) * N_MEM, :] = jnp.where(lane_head == h, v_r, 0.0).astype(BF16)

    for c0 in range(0, m, ROW_CHUNK):
        r, q0 = divmod(c0, tt)
        x = x_ref[r, q0:q0 + ROW_CHUNK, :]
        ms = jnp.mean(x * x, axis=-1, keepdims=True)
        hbuf[c0:c0 + ROW_CHUNK, :] = (x * lax.rsqrt(ms + EPS) * ng_ref[...]).astype(BF16)

    ubuf[...] = jnp.dot(hbuf[...], wperm[...], preferred_element_type=F32)

    sub_pos = pos0 + t * tt + GATHER_STRIDE * lax.broadcasted_iota(jnp.int32, (SUBLANES, 1), 0)
    low_half = lax.broadcasted_iota(jnp.int32, (SUBLANES, LANES), 1) < POOL_CH
    steps = range(GATHER_STRIDE)

    att_rows = min(tt, ATT_ROWS)
    for a0 in range(0, m, att_rows):
        rows = slice(a0, a0 + att_rows)
        qb = (ubuf[rows, _cols(COL_X_Q)] * (MEM_HEAD_DIM ** -0.5)).astype(BF16)
        sbuf[rows, :] = jnp.dot(qb, kbd[a0 // tt], preferred_element_type=F32)
    for a0 in range(0, m, att_rows):
        for c0 in range(a0, a0 + att_rows, ROW_CHUNK):
            rows = slice(c0, c0 + ROW_CHUNK)
            for h in range(MEM_HEADS):
                hc = slice(h * N_MEM, (h + 1) * N_MEM)
                s = sbuf[rows, hc]
                e = jnp.exp(s - jnp.max(s, axis=-1, keepdims=True))
                pbuf[rows, hc] = (e * (1.0 / jnp.sum(e, axis=-1, keepdims=True))).astype(BF16)
        rows = slice(a0, a0 + att_rows)
        ubuf[rows, _cols(COL_X_Q)] = jnp.dot(pbuf[rows, :], vbd[a0 // tt], preferred_element_type=F32)

    block_rows = min(m, BLOCK_ROWS)
    for b0 in range(0, m, block_rows):
        blk = slice(b0, b0 + block_rows)
        chunks = [(c0,) + divmod(c0, tt) for c0 in range(b0, b0 + block_rows, ROW_CHUNK)]

        for c0, r, q0 in chunks:
            rows = slice(c0, c0 + ROW_CHUNK)
            v = ubuf[rows, _cols(COL_A_C)] * ubuf[rows, _cols(COL_A_X)]
            glu = ubuf[rows, _cols(COL_B_V)] * _sigmoid(ubuf[rows, _cols(COL_B_A)])
            cu = ubuf[rows, _cols(COL_C_U)]
            for c in range(SLABS):
                lanes = slice(c * LANES, (c + 1) * LANES)
                aext[r, c, PAD_A + q0:PAD_A + q0 + ROW_CHUNK, :] = v[:, lanes]
                bext[r, c, PAD_B + q0:PAD_B + q0 + ROW_CHUNK, :] = glu[:, lanes]
                pext[r, c, PAD_P + q0:PAD_P + q0 + ROW_CHUNK, :] = cu[:, lanes]

        for out_row, r, q0 in chunks:
            for c in range(SLABS):
                lanes = slice(c * LANES, (c + 1) * LANES)

                acc = [None] * GATHER_STRIDE
                for k in range(CONV_A_W):
                    w = wa[k, :, lanes]
                    for j in steps:
                        term = w * _gather(aext.at[r, c], PAD_A - HIST_A + k + q0 + j)
                        acc[j] = term if k == 0 else acc[j] + term
                for j in steps:
                    mixed[0, c, pl.ds(out_row + j, SUBLANES, stride=GATHER_STRIDE), :] = acc[j]

                acc = [None] * GATHER_STRIDE
                for k in range(CONV_B_W):
                    w = wb[k, :, lanes]
                    for j in steps:
                        term = w * _gather(bext.at[r, c], PAD_B - HIST_B + k + q0 + j)
                        acc[j] = term if k == 0 else acc[j] + term
                for j in steps:
                    mixed[1, c, pl.ds(out_row + j, SUBLANES, stride=GATHER_STRIDE), :] = acc[j] + cbb_ref[:, lanes]

                w_near, w_far = POOL_WINDOWS[2 * c], POOL_WINDOWS[2 * c + 1]
                for j in steps:
                    cu = _gather(pext.at[r, c], PAD_P + q0 + j)
                    near = cu
                    for d in range(1, w_near):
                        near = near + _gather(pext.at[r, c], PAD_P + q0 + j - d)
                    far = near
                    for d in range(w_near, w_far):
                        far = far + _gather(pext.at[r, c], PAD_P + q0 + j - d)
                    pos = sub_pos + (q0 + j)
                    inv_near = 1.0 / jnp.minimum(w_near, pos + 1).astype(F32)
                    inv_far = 1.0 / jnp.minimum(w_far, pos + 1).astype(F32)
                    pooled = jnp.where(low_half, near * inv_near, far * inv_far)
                    mixed[2, c, pl.ds(out_row + j, SUBLANES, stride=GATHER_STRIDE), :] = pooled - cu

        for c0, _, _ in chunks:
            rows = slice(c0, c0 + ROW_CHUNK)

            def mixed_rows(g):
                return jnp.concatenate([mixed[g, c, rows, :] for c in range(SLABS)], axis=-1)

            y_a = ubuf[rows, _cols(COL_A_B)] * mixed_rows(0) * _silu(ubuf[rows, _cols(COL_A_G)])
            ybuf[rows, _cols(0)] = y_a.astype(BF16)

            z = mixed_rows(1)
            mu = jnp.mean(z, axis=-1, keepdims=True)
            zc = z - mu
            var = jnp.mean(zc * zc, axis=-1, keepdims=True)
            zn = zc * lax.rsqrt(var + EPS) * lng_ref[...] + lnb_ref[...]
            y_b = _silu(zn) * _silu(ubuf[rows, _cols(COL_B_G)])
            ybuf[rows, _cols(1)] = y_b.astype(BF16)

            dbuf[rows, :] = mixed_rows(2).astype(BF16)

        ubuf[blk, _cols(COL_C_U)] = jnp.dot(dbuf[blk, :], pw_ref[...], preferred_element_type=F32)

        for c0, _, _ in chunks:
            rows = slice(c0, c0 + ROW_CHUNK)
            y_c = (ubuf[rows, _cols(COL_C_U)] + pb_ref[...]) * ps_ref[...] * _silu(ubuf[rows, _cols(COL_C_G)])
            ybuf[rows, _cols(2)] = y_c.astype(BF16)
            y_x = ubuf[rows, _cols(COL_X_Q)] * _silu(ubuf[rows, _cols(COL_X_G)])
            ybuf[rows, _cols(3)] = y_x.astype(BF16)

        ubuf[blk, 0:D_MODEL] = jnp.dot(ybuf[blk, :], wout_ref[...], preferred_element_type=F32)
        for c0, r, q0 in chunks:
            out = x_ref[r, q0:q0 + ROW_CHUNK, :] + ubuf[c0:c0 + ROW_CHUNK, 0:D_MODEL]
            if final_norm:
                ms = jnp.mean(out * out, axis=-1, keepdims=True)
                out = out * lax.rsqrt(ms + EPS) * fg_ref[...]
            y_ref[r, q0:q0 + ROW_CHUNK, :] = out

    for ext, _, new_ref, pad, hn in hist:
        for r in range(nb):
            for c in range(SLABS):
                tail = ext[r, c, pad + tt - hn:pad + tt, :]
                ext[r, c, pad - hn:pad, :] = tail

                @pl.when(t == nt - 1)
                def _():
                    new_ref[r, :, c * LANES:(c + 1) * LANES] = tail


def _layer_call(layer, x, states, mem_k, mem_v, params, *, tile_rows, nb, pos0, final_norm):
    b, t_len, _ = x.shape
    tt = tile_rows // nb
    grid = (b // nb, t_len // tt)
    m = tile_rows
    has_state = states is not None

    def layer_block(a):
        index_map = lambda i, j: (layer,) + (0,) * (a.ndim - 1)
        if a.size // a.shape[0] >= D_MODEL * D_MODEL:
            return pl.BlockSpec((None,) + a.shape[1:], index_map, pipeline_mode=pl.Buffered(1))
        return pl.BlockSpec((None,) + a.shape[1:], index_map)

    def state_block(hn):
        return pl.BlockSpec((None, nb, hn, GROUP_W), lambda i, j: (layer, i, 0, 0))

    def new_state_block(hn):
        return pl.BlockSpec((nb, hn, GROUP_W), lambda i, j: (i, 0, 0))

    x_spec = pl.BlockSpec((nb, tt, D_MODEL), lambda i, j: (i, j, 0))
    kv_spec = pl.BlockSpec((None, nb, N_MEM, GROUP_W), lambda i, j: (layer, i, 0, 0))
    hists = (HIST_A, HIST_B, HIST_P)
    inputs, in_specs = [x], [x_spec]
    if has_state:
        inputs += list(states)
        in_specs += [state_block(hn) for hn in hists]
    inputs += [mem_k, mem_v]
    in_specs += [kv_spec, kv_spec]
    inputs += list(params)
    in_specs += [layer_block(p) for p in params[:-1]] + [pl.BlockSpec(params[-1].shape, lambda i, j: (0, 0))]

    out_specs = [x_spec] + [new_state_block(hn) for hn in hists]
    out_shape = [jax.ShapeDtypeStruct(x.shape, F32)] + [jax.ShapeDtypeStruct((b, hn, GROUP_W), F32) for hn in hists]
    scratch = [pltpu.VMEM((nb, SLABS, pad + tt, LANES), F32) for pad in (PAD_A, PAD_B, PAD_P)]
    scratch += [pltpu.VMEM((CONV_A_W, SUBLANES, GROUP_W), F32),
                pltpu.VMEM((CONV_B_W, SUBLANES, GROUP_W), F32),
                pltpu.VMEM((D_MODEL, IN_COLS), BF16),
                pltpu.VMEM((3, SLABS, m, LANES), F32),
                pltpu.VMEM((m, D_MODEL), BF16),
                pltpu.VMEM((m, IN_COLS), F32),
                pltpu.VMEM((m, GROUP_W), BF16),
                pltpu.VMEM((m, MEM_HEADS * N_MEM), F32),
                pltpu.VMEM((m, MEM_HEADS * N_MEM), BF16),
                pltpu.VMEM((m, D_MODEL), BF16),
                pltpu.VMEM((nb, GROUP_W, MEM_HEADS * N_MEM), BF16),
                pltpu.VMEM((nb, MEM_HEADS * N_MEM, GROUP_W), BF16)]
    body = functools.partial(_layer_kernel, nb=nb, tt=tt, pos0=pos0, has_state=has_state, final_norm=final_norm)
    return pl.pallas_call(
        body, grid=grid, in_specs=in_specs, out_specs=out_specs, out_shape=out_shape, scratch_shapes=scratch,
        compiler_params=pltpu.CompilerParams(dimension_semantics=("arbitrary", "arbitrary"),
                                             vmem_limit_bytes=VMEM_LIMIT_BYTES),
        name="encoder_layer",
    )(*inputs)


def _memkv_kernel(mem_ref, g_ref, wk_ref, wv_ref, k_ref, v_ref):
    x = mem_ref[...]
    ms = jnp.mean(x * x, axis=-1, keepdims=True)
    mn = (x * lax.rsqrt(ms + EPS) * g_ref[...]).astype(BF16)
    k_ref[...] = jnp.dot(mn, wk_ref[...], preferred_element_type=F32)
    v_ref[...] = jnp.dot(mn, wv_ref[...], preferred_element_type=F32)


def _memkv_call(mem2d, g, wk, wv):
    depth = g.shape[0]
    rows = mem2d.shape[0]
    out = jax.ShapeDtypeStruct((depth, rows, GROUP_W), F32)
    return pl.pallas_call(
        _memkv_kernel, grid=(depth,),
        in_specs=[pl.BlockSpec((rows, D_MODEL), lambda l: (0, 0)),
                  pl.BlockSpec((None, 1, D_MODEL), lambda l: (l, 0, 0)),
                  pl.BlockSpec((None, D_MODEL, GROUP_W), lambda l: (l, 0, 0)),
                  pl.BlockSpec((None, D_MODEL, GROUP_W), lambda l: (l, 0, 0))],
        out_specs=[pl.BlockSpec((None, rows, GROUP_W), lambda l: (l, 0, 0)),
                   pl.BlockSpec((None, rows, GROUP_W), lambda l: (l, 0, 0))],
        out_shape=[out, out],
        compiler_params=pltpu.CompilerParams(dimension_semantics=("arbitrary",), vmem_limit_bytes=VMEM_LIMIT_BYTES),
        name="memory_kv",
    )(mem2d, g, wk, wv)


def _pool_block_diag(pool_w):
    depth, g = pool_w.shape[:2]
    eye = jnp.eye(g, dtype=pool_w.dtype)
    return jnp.einsum('lgce,gh->lgche', pool_w, eye).reshape(depth, g * POOL_CH, g * POOL_CH)


def kernel(x_prompt, x_sample, mem_prompt, state_conv_a, state_conv_b, state_pool, cache_mem_k, cache_mem_v,
           norm_g, w_in, conv_a_w, conv_b_w, conv_b_bias, ln_b_g, ln_b_b, pool_w, pool_bias, pool_scale,
           mem_norm_g, w_mem_k, w_mem_v, w_out, final_norm_g):
    depth = norm_g.shape[0]
    bp, seq, _ = x_prompt.shape
    bs, dec_seq, _ = x_sample.shape
    sample_nb = SAMPLE_TILE_ROWS // dec_seq
    assert seq % PROMPT_TILE_ROWS == 0 and SAMPLE_TILE_ROWS % dec_seq == 0 and bs % sample_nb == 0
    assert dec_seq % ROW_CHUNK == 0 and BLOCK_ROWS % dec_seq == 0
    assert PROMPT_TILE_ROWS % max(BLOCK_ROWS, ATT_ROWS) == 0 and SAMPLE_TILE_ROWS % min(BLOCK_ROWS, SAMPLE_TILE_ROWS) == 0

    mk_all, mv_all = _memkv_call(mem_prompt.reshape(bp * N_MEM, D_MODEL), mem_norm_g[:, None, :],
                                 w_mem_k.astype(BF16), w_mem_v.astype(BF16))
    mk4 = mk_all.reshape(depth, bp, N_MEM, GROUP_W)
    mv4 = mv_all.reshape(depth, bp, N_MEM, GROUP_W)
    cache_k = cache_mem_k.reshape(depth, bs, N_MEM, GROUP_W)
    cache_v = cache_mem_v.reshape(depth, bs, N_MEM, GROUP_W)

    def row(a):
        return a[:, None, :]

    params = (row(norm_g), w_in.astype(BF16), conv_a_w, conv_b_w, row(conv_b_bias), row(ln_b_g), row(ln_b_b),
              _pool_block_diag(pool_w).astype(BF16), row(pool_bias), row(pool_scale), w_out.astype(BF16),
              final_norm_g[None, :])
    states = (state_conv_a, state_conv_b, state_pool)

    xp, xs = x_prompt, x_sample
    new_p, new_s = [], []
    for l in range(depth):
        last = l == depth - 1
        xp, *st = _layer_call(l, xp, None, mk4, mv4, params, tile_rows=PROMPT_TILE_ROWS, nb=1, pos0=0,
                              final_norm=last)
        new_p.append(st)
        xs, *st = _layer_call(l, xs, states, cache_k, cache_v, params, tile_rows=SAMPLE_TILE_ROWS, nb=sample_nb,
                              pos0=PAST_LEN, final_norm=last)
        new_s.append(st)

    def stack(new, i):
        return jnp.stack([st[i] for st in new])

    mk_out = mk_all.reshape(depth, bp, N_MEM, MEM_HEADS, MEM_HEAD_DIM)
    mv_out = mv_all.reshape(depth, bp, N_MEM, MEM_HEADS, MEM_HEAD_DIM)
    return (xp, xs, stack(new_p, 0), stack(new_p, 1), stack(new_p, 2), mk_out, mv_out,
            stack(new_s, 0), stack(new_s, 1), stack(new_s, 2))
```

```python
import functools

import jax
import jax.numpy as jnp
from jax import lax
from jax.experimental import pallas as pl
from jax.experimental.pallas import tpu as pltpu

D_MODEL = 1024
GROUP_W = 256
IN_COLS = 11 * GROUP_W
CONV_A_W = 3
CONV_B_W = 31
POOL_WINDOWS = (2, 4, 8, 16)
POOL_CH = 64
HIST_A = CONV_A_W - 1
HIST_B = CONV_B_W - 1
HIST_P = max(POOL_WINDOWS) - 1
N_MEM = 256
MEM_HEADS = 4
MEM_HEAD_DIM = 64
EPS = 1e-6
PAST_LEN = 1024

PROJ_PERM = (4, 5, 1, 2, 7, 9, 0, 3, 6, 8, 10)
COL_B_V, COL_B_A, COL_A_C, COL_A_X, COL_C_U, COL_X_Q, COL_A_B, COL_A_G, COL_B_G, COL_C_G, COL_X_G = range(11)

LANES = 128
SUBLANES = 8
SLABS = GROUP_W // LANES
PAD_A = 8
PAD_B = 32
PAD_P = 16
ROW_CHUNK = 32
GATHER_STRIDE = ROW_CHUNK // SUBLANES
PROMPT_TILE_ROWS = 1024
SAMPLE_TILE_ROWS = 512
BLOCK_ROWS = 1024
ATT_ROWS = 512
VMEM_LIMIT_BYTES = 56 * 1024 * 1024

F32 = jnp.float32
BF16 = jnp.bfloat16


def _cols(g):
    return slice(g * GROUP_W, (g + 1) * GROUP_W)


def _sigmoid(x):
    return 1.0 / (1.0 + jnp.exp(-x))


def _silu(x):
    return x * _sigmoid(x)


def _gather(slab, base):
    return slab[pl.ds(base, SUBLANES, stride=GATHER_STRIDE), :]


def _layer_kernel(*refs, nb, tt, pos0, has_state, final_norm):
    if has_state:
        x_ref, sa_ref, sb_ref, sp_ref, k_ref, v_ref = refs[:6]
        rest = refs[6:]
    else:
        x_ref, k_ref, v_ref = refs[:3]
        sa_ref = sb_ref = sp_ref = None
        rest = refs[3:]
    (ng_ref, win_ref, caw_ref, cbw_ref, cbb_ref, lng_ref, lnb_ref, pw_ref, pb_ref, ps_ref, wout_ref, fg_ref,
     y_ref, na_ref, nb_ref, np_ref,
     aext, bext, pext, wa, wb, wperm, mixed, hbuf, ubuf, dbuf, sbuf, pbuf, ybuf, kbd, vbd) = rest
    t = pl.program_id(1)
    nt = pl.num_programs(1)
    m = nb * tt
    hist = ((aext, sa_ref, na_ref, PAD_A, HIST_A), (bext, sb_ref, nb_ref, PAD_B, HIST_B),
            (pext, sp_ref, np_ref, PAD_P, HIST_P))

    @pl.when((pl.program_id(0) == 0) & (t == 0))
    def _():
        for j, g in enumerate(PROJ_PERM):
            wperm[:, _cols(j)] = win_ref[:, _cols(g)]

    @pl.when(t == 0)
    def _():
        for ext, st_ref, _, pad, hn in hist:
            for r in range(nb):
                for c in range(SLABS):
                    lanes = slice(c * LANES, (c + 1) * LANES)
                    ext[r, c, pad - hn:pad, :] = (st_ref[r, :, lanes] if has_state else jnp.zeros((hn, LANES), F32))
        for k in range(CONV_A_W):
            wa[k] = jnp.broadcast_to(caw_ref[k:k + 1, :], (SUBLANES, GROUP_W))
        for k in range(CONV_B_W):
            wb[k] = jnp.broadcast_to(cbw_ref[k:k + 1, :], (SUBLANES, GROUP_W))
        row_head = lax.broadcasted_iota(jnp.int32, (GROUP_W, N_MEM), 0) // MEM_HEAD_DIM
        lane_head = lax.broadcasted_iota(jnp.int32, (N_MEM, GROUP_W), 1) // MEM_HEAD_DIM
        for r in range(nb):
            k_t = k_ref[r].astype(F32).T
            v_r = v_ref[r].astype(F32)
            for h in range(MEM_HEADS):
                kbd[r, :, h * N_MEM:(h + 1) * N_MEM] = jnp.where(row_head == h, k_t, 0.0).astype(BF16)
                vbd[r, h * N_MEM:(h + 1) * N_MEM, :] = jnp.where(lane_head == h, v_r, 0.0).astype(BF16)

    for c0 in range(0, m, ROW_CHUNK):
        r, q0 = divmod(c0, tt)
        x = x_ref[r, q0:q0 + ROW_CHUNK, :]
        ms = jnp.mean(x * x, axis=-1, keepdims=True)
        hbuf[c0:c0 + ROW_CHUNK, :] = (x * lax.rsqrt(ms + EPS) * ng_ref[...]).astype(BF16)

    ubuf[...] = jnp.dot(hbuf[...], wperm[...], preferred_element_type=F32)

    sub_pos = pos0 + t * tt + GATHER_STRIDE * lax.broadcasted_iota(jnp.int32, (SUBLANES, 1), 0)
    low_half = lax.broadcasted_iota(jnp.int32, (SUBLANES, LANES), 1) < POOL_CH
    steps = range(GATHER_STRIDE)

    att_rows = min(tt, ATT_ROWS)
    for a0 in range(0, m, att_rows):
        rows = slice(a0, a0 + att_rows)
        qb = (ubuf[rows, _cols(COL_X_Q)] * (MEM_HEAD_DIM ** -0.5)).astype(BF16)
        sbuf[rows, :] = jnp.dot(qb, kbd[a0 // tt], preferred_element_type=F32)
    for a0 in range(0, m, att_rows):
        for c0 in range(a0, a0 + att_rows, ROW_CHUNK):
            rows = slice(c0, c0 + ROW_CHUNK)
            for h in range(MEM_HEADS):
                hc = slice(h * N_MEM, (h + 1) * N_MEM)
                s = sbuf[rows, hc]
                e = jnp.exp(s - jnp.max(s, axis=-1, keepdims=True))
                pbuf[rows, hc] = (e * (1.0 / jnp.sum(e, axis=-1, keepdims=True))).astype(BF16)
        rows = slice(a0, a0 + att_rows)
        ubuf[rows, _cols(COL_X_Q)] = jnp.dot(pbuf[rows, :], vbd[a0 // tt], preferred_element_type=F32)

    block_rows = min(m, BLOCK_ROWS)
    for b0 in range(0, m, block_rows):
        blk = slice(b0, b0 + block_rows)
        chunks = [(c0,) + divmod(c0, tt) for c0 in range(b0, b0 + block_rows, ROW_CHUNK)]

        for c0, r, q0 in chunks:
            rows = slice(c0, c0 + ROW_CHUNK)
            v = ubuf[rows, _cols(COL_A_C)] * ubuf[rows, _cols(COL_A_X)]
            glu = ubuf[rows, _cols(COL_B_V)] * _sigmoid(ubuf[rows, _cols(COL_B_A)])
            cu = ubuf[rows, _cols(COL_C_U)]
            for c in range(SLABS):
                lanes = slice(c * LANES, (c + 1) * LANES)
                aext[r, c, PAD_A + q0:PAD_A + q0 + ROW_CHUNK, :] = v[:, lanes]
                bext[r, c, PAD_B + q0:PAD_B + q0 + ROW_CHUNK, :] = glu[:, lanes]
                pext[r, c, PAD_P + q0:PAD_P + q0 + ROW_CHUNK, :] = cu[:, lanes]

        for out_row, r, q0 in chunks:
            for c in range(SLABS):
                lanes = slice(c * LANES, (c + 1) * LANES)

                acc = [None] * GATHER_STRIDE
                for k in range(CONV_A_W):
                    w = wa[k, :, lanes]
                    for j in steps:
                        term = w * _gather(aext.at[r, c], PAD_A - HIST_A + k + q0 + j)
                        acc[j] = term if k == 0 else acc[j] + term
                for j in steps:
                    mixed[0, c, pl.ds(out_row + j, SUBLANES, stride=GATHER_STRIDE), :] = acc[j]

                acc = [None] * GATHER_STRIDE
                for k in range(CONV_B_W):
                    w = wb[k, :, lanes]
                    for j in steps:
                        term = w * _gather(bext.at[r, c], PAD_B - HIST_B + k + q0 + j)
                        acc[j] = term if k == 0 else acc[j] + term
                for j in steps:
                    mixed[1, c, pl.ds(out_row + j, SUBLANES, stride=GATHER_STRIDE), :] = acc[j] + cbb_ref[:, lanes]

                w_near, w_far = POOL_WINDOWS[2 * c], POOL_WINDOWS[2 * c + 1]
                for j in steps:
                    cu = _gather(pext.at[r, c], PAD_P + q0 + j)
                    near = cu
                    for d in range(1, w_near):
                        near = near + _gather(pext.at[r, c], PAD_P + q0 + j - d)
                    far = near
                    for d in range(w_near, w_far):
                        far = far + _gather(pext.at[r, c], PAD_P + q0 + j - d)
                    pos = sub_pos + (q0 + j)
                    inv_near = 1.0 / jnp.minimum(w_near, pos + 1).astype(F32)
                    inv_far = 1.0 / jnp.minimum(w_far, pos + 1).astype(F32)
                    pooled = jnp.where(low_half, near * inv_near, far * inv_far)
                    mixed[2, c, pl.ds(out_row + j, SUBLANES, stride=GATHER_STRIDE), :] = pooled - cu

        for c0, _, _ in chunks:
            rows = slice(c0, c0 + ROW_CHUNK)

            def mixed_rows(g):
                return jnp.concatenate([mixed[g, c, rows, :] for c in range(SLABS)], axis=-1)

            y_a = ubuf[rows, _cols(COL_A_B)] * mixed_rows(0) * _silu(ubuf[rows, _cols(COL_A_G)])
            ybuf[rows, _cols(0)] = y_a.astype(BF16)

            z = mixed_rows(1)
            mu = jnp.mean(z, axis=-1, keepdims=True)
            zc = z - mu
            var = jnp.mean(zc * zc, axis=-1, keepdims=True)
            zn = zc * lax.rsqrt(var + EPS) * lng_ref[...] + lnb_ref[...]
            y_b = _silu(zn) * _silu(ubuf[rows, _cols(COL_B_G)])
            ybuf[rows, _cols(1)] = y_b.astype(BF16)

            dbuf[rows, :] = mixed_rows(2).astype(BF16)

        ubuf[blk, _cols(COL_C_U)] = jnp.dot(dbuf[blk, :], pw_ref[...], preferred_element_type=F32)

        for c0, _, _ in chunks:
            rows = slice(c0, c0 + ROW_CHUNK)
            y_c = (ubuf[rows, _cols(COL_C_U)] + pb_ref[...]) * ps_ref[...] * _silu(ubuf[rows, _cols(COL_C_G)])
            ybuf[rows, _cols(2)] = y_c.astype(BF16)
            y_x = ubuf[rows, _cols(COL_X_Q)] * _silu(ubuf[rows, _cols(COL_X_G)])
            ybuf[rows, _cols(3)] = y_x.astype(BF16)

        ubuf[blk, 0:D_MODEL] = jnp.dot(ybuf[blk, :], wout_ref[...], preferred_element_type=F32)
        for c0, r, q0 in chunks:
            out = x_ref[r, q0:q0 + ROW_CHUNK, :] + ubuf[c0:c0 + ROW_CHUNK, 0:D_MODEL]
            if final_norm:
                ms = jnp.mean(out * out, axis=-1, keepdims=True)
                out = out * lax.rsqrt(ms + EPS) * fg_ref[...]
            y_ref[r, q0:q0 + ROW_CHUNK, :] = out

    for ext, _, new_ref, pad, hn in hist:
        for r in range(nb):
            for c in range(SLABS):
                tail = ext[r, c, pad + tt - hn:pad + tt, :]
                ext[r, c, pad - hn:pad, :] = tail

                @pl.when(t == nt - 1)
                def _():
                    new_ref[r, :, c * LANES:(c + 1) * LANES] = tail


def _layer_call(layer, x, states, mem_k, mem_v, params, *, tile_rows, nb, pos0, final_norm):
    b, t_len, _ = x.shape
    tt = tile_rows // nb
    grid = (b // nb, t_len // tt)
    m = tile_rows
    has_state = states is not None

    def layer_block(a):
        index_map = lambda i, j: (layer,) + (0,) * (a.ndim - 1)
        if a.size // a.shape[0] >= D_MODEL * D_MODEL:
            return pl.BlockSpec((None,) + a.shape[1:], index_map, pipeline_mode=pl.Buffered(1))
        return pl.BlockSpec((None,) + a.shape[1:], index_map)

    def state_block(hn):
        return pl.BlockSpec((None, nb, hn, GROUP_W), lambda i, j: (layer, i, 0, 0))

    def new_state_block(hn):
        return pl.BlockSpec((nb, hn, GROUP_W), lambda i, j: (i, 0, 0))

    x_spec = pl.BlockSpec((nb, tt, D_MODEL), lambda i, j: (i, j, 0))
    kv_spec = pl.BlockSpec((None, nb, N_MEM, GROUP_W), lambda i, j: (layer, i, 0, 0))
    hists = (HIST_A, HIST_B, HIST_P)
    inputs, in_specs = [x], [x_spec]
    if has_state:
        inputs += list(states)
        in_specs += [state_block(hn) for hn in hists]
    inputs += [mem_k, mem_v]
    in_specs += [kv_spec, kv_spec]
    inputs += list(params)
    in_specs += [layer_block(p) for p in params[:-1]] + [pl.BlockSpec(params[-1].shape, lambda i, j: (0, 0))]

    out_specs = [x_spec] + [new_state_block(hn) for hn in hists]
    out_shape = [jax.ShapeDtypeStruct(x.shape, F32)] + [jax.ShapeDtypeStruct((b, hn, GROUP_W), F32) for hn in hists]
    scratch = [pltpu.VMEM((nb, SLABS, pad + tt, LANES), F32) for pad in (PAD_A, PAD_B, PAD_P)]
    scratch += [pltpu.VMEM((CONV_A_W, SUBLANES, GROUP_W), F32),
                pltpu.VMEM((CONV_B_W, SUBLANES, GROUP_W), F32),
                pltpu.VMEM((D_MODEL, IN_COLS), BF16),
                pltpu.VMEM((3, SLABS, m, LANES), F32),
                pltpu.VMEM((m, D_MODEL), BF16),
                pltpu.VMEM((m, IN_COLS), F32),
                pltpu.VMEM((m, GROUP_W), BF16),
                pltpu.VMEM((m, MEM_HEADS * N_MEM), F32),
                pltpu.VMEM((m, MEM_HEADS * N_MEM), BF16),
                pltpu.VMEM((m, D_MODEL), BF16),
                pltpu.VMEM((nb, GROUP_W, MEM_HEADS * N_MEM), BF16),
                pltpu.VMEM((nb, MEM_HEADS * N_MEM, GROUP_W), BF16)]
    body = functools.partial(_layer_kernel, nb=nb, tt=tt, pos0=pos0, has_state=has_state, final_norm=final_norm)
    return pl.pallas_call(
        body, grid=grid, in_specs=in_specs, out_specs=out_specs, out_shape=out_shape, scratch_shapes=scratch,
        compiler_params=pltpu.CompilerParams(dimension_semantics=("arbitrary", "arbitrary"),
                                             vmem_limit_bytes=VMEM_LIMIT_BYTES),
        name="encoder_layer",
    )(*inputs)


def _memkv_kernel(mem_ref, g_ref, wk_ref, wv_ref, k_ref, v_ref):
    x = mem_ref[...]
    ms = jnp.mean(x * x, axis=-1, keepdims=True)
    mn = (x * lax.rsqrt(ms + EPS) * g_ref[...]).astype(BF16)
    k_ref[...] = jnp.dot(mn, wk_ref[...], preferred_element_type=F32)
    v_ref[...] = jnp.dot(mn, wv_ref[...], preferred_element_type=F32)


def _memkv_call(mem2d, g, wk, wv):
    depth = g.shape[0]
    rows = mem2d.shape[0]
    out = jax.ShapeDtypeStruct((depth, rows, GROUP_W), F32)
    return pl.pallas_call(
        _memkv_kernel, grid=(depth,),
        in_specs=[pl.BlockSpec((rows, D_MODEL), lambda l: (0, 0)),
                  pl.BlockSpec((None, 1, D_MODEL), lambda l: (l, 0, 0)),
                  pl.BlockSpec((None, D_MODEL, GROUP_W), lambda l: (l, 0, 0)),
                  pl.BlockSpec((None, D_MODEL, GROUP_W), lambda l: (l, 0, 0))],
        out_specs=[pl.BlockSpec((None, rows, GROUP_W), lambda l: (l, 0, 0)),
                   pl.BlockSpec((None, rows, GROUP_W), lambda l: (l, 0, 0))],
        out_shape=[out, out],
        compiler_params=pltpu.CompilerParams(dimension_semantics=("arbitrary",), vmem_limit_bytes=VMEM_LIMIT_BYTES),
        name="memory_kv",
    )(mem2d, g, wk, wv)


def _pool_block_diag(pool_w):
    depth, g = pool_w.shape[:2]
    eye = jnp.eye(g, dtype=pool_w.dtype)
    return jnp.einsum('lgce,gh->lgche', pool_w, eye).reshape(depth, g * POOL_CH, g * POOL_CH)


def kernel(x_prompt, x_sample, mem_prompt, state_conv_a, state_conv_b, state_pool, cache_mem_k, cache_mem_v,
           norm_g, w_in, conv_a_w, conv_b_w, conv_b_bias, ln_b_g, ln_b_b, pool_w, pool_bias, pool_scale,
           mem_norm_g, w_mem_k, w_mem_v, w_out, final_norm_g):
    depth = norm_g.shape[0]
    bp, seq, _ = x_prompt.shape
    bs, dec_seq, _ = x_sample.shape
    sample_nb = SAMPLE_TILE_ROWS // dec_seq
    assert seq % PROMPT_TILE_ROWS == 0 and SAMPLE_TILE_ROWS % dec_seq == 0 and bs % sample_nb == 0
    assert dec_seq % ROW_CHUNK == 0 and BLOCK_ROWS % dec_seq == 0
    assert PROMPT_TILE_ROWS % max(BLOCK_ROWS, ATT_ROWS) == 0 and SAMPLE_TILE_ROWS % min(BLOCK_ROWS, SAMPLE_TILE_ROWS) == 0

    mk_all, mv_all = _memkv_call(mem_prompt.reshape(bp * N_MEM, D_MODEL), mem_norm_g[:, None, :],
                                 w_mem_k.astype(BF16), w_mem_v.astype(BF16))
    mk4 = mk_all.reshape(depth, bp, N_MEM, GROUP_W)
    mv4 = mv_all.reshape(depth, bp, N_MEM, GROUP_W)
    cache_k = cache_mem_k.astype(BF16).reshape(depth, bs, N_MEM, GROUP_W)
    cache_v = cache_mem_v.astype(BF16).reshape(depth, bs, N_MEM, GROUP_W)

    def row(a):
        return a[:, None, :]

    params = (row(norm_g), w_in.astype(BF16), conv_a_w, conv_b_w, row(conv_b_bias), row(ln_b_g), row(ln_b_b),
              _pool_block_diag(pool_w).astype(BF16), row(pool_bias), row(pool_scale), w_out.astype(BF16),
              final_norm_g[None, :])
    states = (state_conv_a, state_conv_b, state_pool)

    xp, xs = x_prompt, x_sample
    new_p, new_s = [], []
    for l in range(depth):
        last = l == depth - 1
        xp, *st = _layer_call(l, xp, None, mk4, mv4, params, tile_rows=PROMPT_TILE_ROWS, nb=1, pos0=0,
                              final_norm=last)
        new_p.append(st)
        xs, *st = _layer_call(l, xs, states, cache_k, cache_v, params, tile_rows=SAMPLE_TILE_ROWS, nb=sample_nb,
                              pos0=PAST_LEN, final_norm=last)
        new_s.append(st)

    def stack(new, i):
        return jnp.stack([st[i] for st in new])

    mk_out = mk_all.reshape(depth, bp, N_MEM, MEM_HEADS, MEM_HEAD_DIM)
    mv_out = mv_all.reshape(depth, bp, N_MEM, MEM_HEADS, MEM_HEAD_DIM)
    return (xp, xs, stack(new_p, 0), stack(new_p, 1), stack(new_p, 2), mk_out, mv_out,
            stack(new_s, 0), stack(new_s, 1), stack(new_s, 2))
```
